```python
import jax, jax.numpy as jnp
from jax import lax
import numpy as np

D_MODEL = 1024
BATCH = 16
SEQ = 4096
DEPTH = 4

RET_HEADS = 4
RET_DH = 128
RET_W = RET_HEADS * RET_DH
CHUNK = 128
ROPE_BASE = 10000.0
POOL_WINDOWS = (2, 4, 8, 16)
POOL_GROUPS = 4
POOL_GW = 128
POOL_W = POOL_GROUPS * POOL_GW
RG_BLOCKS = 4
RG_BD = 128
RG_W = RG_BLOCKS * RG_BD
RG_CONV = 4
RG_C = 8.0
N_BRANCH = 3
SPLITS = (RET_W, RET_W, RET_W, RET_W, POOL_W, RG_W, RG_W, N_BRANCH * D_MODEL)
IN_W = sum(SPLITS)
N_EXPERTS = 64
N_GROUPS = 8
E_PER_GROUP = N_EXPERTS // N_GROUPS
TOPK_GROUPS = 4
TOP_K = 8
D_EXPERT = 256
ROUTED_SCALE = 2.5
EXPERT_BLOCK = 256
LN_EPS = 1e-5
GN_EPS = 1e-6
ALPHA = (2 * DEPTH) ** 0.25
BETA = (8 * DEPTH) ** -0.25

kernel_name = "hybrid_retention_pool_rglru_moe_deepnorm"


def layer_norm(x, g, b):
    xf = x.astype(jnp.float32)
    mu = xf.mean(-1, keepdims=True)
    var = jnp.square(xf - mu).mean(-1, keepdims=True)
    return ((xf - mu) * lax.rsqrt(var + LN_EPS) * g.astype(jnp.float32) + b.astype(jnp.float32)).astype(x.dtype)


def rope_tables(positions):
    half = RET_DH // 2
    inv = ROPE_BASE ** (-jnp.arange(half, dtype=jnp.float32) / half)
    ang = positions.astype(jnp.float32)[:, None] * inv[None, :]
    return jnp.cos(ang), jnp.sin(ang)


def apply_rope(t, cos, sin):
    half = RET_DH // 2
    t1, t2 = t[..., :half], t[..., half:]
    return jnp.concatenate([t1 * cos - t2 * sin, t1 * sin + t2 * cos], axis=-1)


def retention(q, k, v, g, cos, sin):
    B, S, _ = q.shape
    N = S // CHUNK
    f32 = jnp.float32

    def heads(t):
        return t.astype(f32).reshape(B, S, RET_HEADS, RET_DH).transpose(0, 2, 1, 3)

    qh = apply_rope(heads(q), cos, sin)
    kh = apply_rope(heads(k), cos, sin) * (RET_DH ** -0.5)
    vh = heads(v)
    qc = qh.reshape(B, RET_HEADS, N, CHUNK, RET_DH)
    kc = kh.reshape(B, RET_HEADS, N, CHUNK, RET_DH)
    vc = vh.reshape(B, RET_HEADS, N, CHUNK, RET_DH)

    log_g = jnp.log(1.0 - 2.0 ** (-5.0 - jnp.arange(RET_HEADS, dtype=f32)))
    idx = jnp.arange(CHUNK, dtype=f32)
    diff = idx[:, None] - idx[None, :]
    decay_mask = jnp.where(diff >= 0, jnp.exp(log_g[:, None, None] * jnp.maximum(diff, 0.0)), 0.0)

    scores = jnp.einsum('bhncd,bhnjd->bhncj', qc, kc) * decay_mask[None, :, None]
    o_intra = jnp.einsum('bhncj,bhnjd->bhncd', scores, vc)

    k_decay = jnp.exp(log_g[:, None] * (CHUNK - 1 - idx)[None, :])
    chunk_kv = jnp.einsum('bhncd,bhnce->bhnde', kc * k_decay[None, :, None, :, None], vc)
    chunk_decay = jnp.exp(log_g * CHUNK)[None, :, None, None]

    def step(state, kv):
        return state * chunk_decay + kv, state

    init = jnp.zeros((B, RET_HEADS, RET_DH, RET_DH), f32)
    _, prev = lax.scan(step, init, jnp.moveaxis(chunk_kv, 2, 0))
    prev = jnp.moveaxis(prev, 0, 2)
    q_decay = jnp.exp(log_g[:, None] * (idx + 1.0)[None, :])
    o_cross = jnp.einsum('bhncd,bhnde->bhnce', qc, prev) * q_decay[None, :, None, :, None]

    o = (o_intra + o_cross).reshape(B, RET_HEADS, S, RET_DH)
    mu = o.mean(-1, keepdims=True)
    var = jnp.square(o - mu).mean(-1, keepdims=True)
    o = (o - mu) * lax.rsqrt(var + GN_EPS)
    o = o.transpose(0, 2, 1, 3).reshape(B, S, RET_W)
    return (jax.nn.silu(g.astype(f32)) * o).astype(q.dtype)


def pool_branch(xp, pool_w, pool_scale):
    B, S, _ = xp.shape
    f32 = jnp.float32
    xg = xp.astype(f32).reshape(B, S, POOL_GROUPS, POOL_GW)
    cp = jnp.concatenate([jnp.zeros((B, 1, POOL_GROUPS, POOL_GW), f32), jnp.cumsum(xg, axis=1)], axis=1)
    t = jnp.arange(S)
    means = []
    for gi, w in enumerate(POOL_WINDOWS):
        n = jnp.minimum(w, t + 1)
        win = cp[:, 1:, gi] - jnp.take(cp[:, :, gi], t + 1 - n, axis=1)
        means.append(win / n.astype(f32)[None, :, None])
    pooled = jnp.stack(means, axis=2) - xg
    y = jnp.einsum('bsgc,gce->bsge', pooled, pool_w.astype(f32)).reshape(B, S, POOL_W)
    return (y * pool_scale.astype(f32)).astype(xp.dtype)


def _lin_comb(c1, c2):
    a1, b1 = c1
    a2, b2 = c2
    return a1 * a2, a2 * b1 + b2


def rglru_branch(xr, yr, conv_w, conv_b, wa, ba, wx, bx, lam):
    B, S, _ = xr.shape
    f32 = jnp.float32
    xf = xr.astype(f32)
    conv = conv_b.astype(f32)
    for k in range(RG_CONV):
        shifted = jnp.pad(xf, ((0, 0), (k, 0), (0, 0)))[:, :S]
        conv = conv + shifted * conv_w[k].astype(f32)
    cb = conv.reshape(B, S, RG_BLOCKS, RG_BD)
    r = jax.nn.sigmoid(jnp.einsum('bsnd,nde->bsne', cb, wa.astype(f32)).reshape(B, S, RG_W) + ba)
    i = jax.nn.sigmoid(jnp.einsum('bsnd,nde->bsne', cb, wx.astype(f32)).reshape(B, S, RG_W) + bx)
    log_a = RG_C * r * jax.nn.log_sigmoid(lam.astype(f32))
    a = jnp.exp(log_a)
    b = jnp.sqrt(-jnp.expm1(2.0 * log_a)) * (i * conv)
    _, h = lax.associative_scan(_lin_comb, (a, b), axis=1)
    return (h * jax.nn.gelu(yr.astype(f32))).astype(xr.dtype)


def token_mixer(x, cos, sin, w_in, gate_b, w_branch, w_out, pool_w, pool_scale,
                conv_w, conv_b, wa, ba, wx, bx, lam):
    B, S, D = x.shape
    h = x @ w_in
    cuts = list(np.cumsum(SPLITS)[:-1])
    q, k, v, g, xp, xr, yr, gl = jnp.split(h, cuts, axis=-1)
    branches = (
        retention(q, k, v, g, cos, sin),
        pool_branch(xp, pool_w, pool_scale),
        rglru_branch(xr, yr, conv_w, conv_b, wa, ba, wx, bx, lam),
    )
    gates = jax.nn.sigmoid((gl.reshape(B, S, N_BRANCH, D) + gate_b).astype(jnp.float32)).astype(x.dtype)
    merged = gates[:, :, 0] * (branches[0] @ w_branch[0])
    for bi in range(1, N_BRANCH):
        merged = merged + gates[:, :, bi] * (branches[bi] @ w_branch[bi])
    return merged @ w_out


def moe_ffn(x2, router_w, router_bias, w1, w3, w2, s_w1, s_w3, s_w2):
    T, D = x2.shape
    scores = jax.nn.sigmoid((x2 @ router_w).astype(jnp.float32))
    biased = scores + router_bias.astype(jnp.float32)
    grp_score = lax.top_k(biased.reshape(T, N_GROUPS, E_PER_GROUP), 2)[0].sum(-1)
    _, top_groups = lax.top_k(grp_score, TOPK_GROUPS)
    gmask = jnp.any(top_groups[:, :, None] == jnp.arange(N_GROUPS)[None, None, :], axis=1)
    masked = jnp.where(jnp.repeat(gmask, E_PER_GROUP, axis=1), biased, -jnp.inf)
    _, top_e = lax.top_k(masked, TOP_K)
    sel = jnp.take_along_axis(scores, top_e, axis=1)
    gate_w = sel / sel.sum(-1, keepdims=True) * ROUTED_SCALE

    A = T * TOP_K
    e_flat = top_e.reshape(A)
    tok_flat = jnp.arange(A, dtype=jnp.int32) // TOP_K
    w_flat = gate_w.reshape(A).astype(x2.dtype)
    order = jnp.argsort(e_flat)
    e_sorted = e_flat[order]
    counts = jnp.bincount(e_flat, length=N_EXPERTS)
    padded = ((counts + EXPERT_BLOCK - 1) // EXPERT_BLOCK) * EXPERT_BLOCK
    pad_end = jnp.cumsum(padded)
    pad_start = pad_end - padded
    start = jnp.cumsum(counts) - counts
    dest = pad_start[e_sorted] + (jnp.arange(A, dtype=jnp.int32) - start[e_sorted])
    P = A + N_EXPERTS * EXPERT_BLOCK
    n_blk = P // EXPERT_BLOCK
    buf_tok = jnp.full((P,), T, jnp.int32).at[dest].set(tok_flat[order])
    buf_w = jnp.zeros((P,), x2.dtype).at[dest].set(w_flat[order])
    blk_start = jnp.arange(n_blk, dtype=jnp.int32) * EXPERT_BLOCK
    blk_expert = jnp.minimum(jnp.searchsorted(pad_end, blk_start, side='right'), N_EXPERTS - 1)
    x_pad = jnp.concatenate([x2, jnp.zeros((1, D), x2.dtype)], axis=0)

    def run_block(args):
        e, tok, wt = args
        xb = x_pad[tok]
        hb = jax.nn.silu(xb @ w1[e]) * (xb @ w3[e])
        return (hb @ w2[e]) * wt[:, None]

    y = lax.map(run_block, (blk_expert, buf_tok.reshape(n_blk, EXPERT_BLOCK), buf_w.reshape(n_blk, EXPERT_BLOCK)))
    routed = jax.ops.segment_sum(y.reshape(P, D), buf_tok, num_segments=T + 1)[:T]
    shared = (jax.nn.silu(x2 @ s_w1) * (x2 @ s_w3)) @ s_w2
    return shared + routed


def setup_inputs(seed: int = 0) -> dict:
    key = jax.random.key(seed)
    ks = jax.random.split(key, 32)
    nrm = jax.random.normal
    L, D = DEPTH, D_MODEL
    f32 = jnp.float32
    a0 = jax.random.uniform(ks[14], (L, RG_W), f32, minval=0.9, maxval=0.999)
    s = a0 ** (1.0 / RG_C)
    return {
        "x": nrm(ks[0], (BATCH, SEQ, D), f32),
        "positions": jnp.arange(SEQ, dtype=jnp.int32),
        "w_in": nrm(ks[1], (L, D, IN_W), f32) * D ** -0.5,
        "gate_b": nrm(ks[2], (L, N_BRANCH, D), f32) * 0.1,
        "w_branch": nrm(ks[3], (L, N_BRANCH, RET_W, D), f32) * (RET_W ** -0.5) * BETA,
        "w_out": nrm(ks[4], (L, D, D), f32) * (D ** -0.5) * BETA,
        "pool_w": nrm(ks[5], (L, POOL_GROUPS, POOL_GW, POOL_GW), f32) * POOL_GW ** -0.5,
        "pool_scale": 1.0 + 0.1 * nrm(ks[6], (L, POOL_W), f32),
        "rg_conv_w": nrm(ks[7], (L, RG_CONV, RG_W), f32) * RG_CONV ** -0.5,
        "rg_conv_b": nrm(ks[8], (L, RG_W), f32) * 0.02,
        "rg_wa": nrm(ks[9], (L, RG_BLOCKS, RG_BD, RG_BD), f32) * RG_BD ** -0.5,
        "rg_ba": nrm(ks[10], (L, RG_W), f32) * 0.02,
        "rg_wx": nrm(ks[11], (L, RG_BLOCKS, RG_BD, RG_BD), f32) * RG_BD ** -0.5,
        "rg_bx": nrm(ks[12], (L, RG_W), f32) * 0.02,
        "rg_lambda": jnp.log(s) - jnp.log1p(-s),
        "ln1_g": 1.0 + 0.02 * nrm(ks[13], (L, D), f32),
        "ln1_b": 0.02 * nrm(ks[15], (L, D), f32),
        "router_w": nrm(ks[16], (L, D, N_EXPERTS), f32) * D ** -0.5,
        "router_bias": nrm(ks[17], (L, N_EXPERTS), f32) * 0.01,
        "exp_w1": nrm(ks[18], (L, N_EXPERTS, D, D_EXPERT), f32) * D ** -0.5,
        "exp_w3": nrm(ks[19], (L, N_EXPERTS, D, D_EXPERT), f32) * D ** -0.5,
        "exp_w2": nrm(ks[20], (L, N_EXPERTS, D_EXPERT, D), f32) * (D_EXPERT ** -0.5) * BETA,
        "sh_w1": nrm(ks[21], (L, D, D_EXPERT), f32) * D ** -0.5,
        "sh_w3": nrm(ks[22], (L, D, D_EXPERT), f32) * D ** -0.5,
        "sh_w2": nrm(ks[23], (L, D_EXPERT, D), f32) * (D_EXPERT ** -0.5) * BETA,
        "ln2_g": 1.0 + 0.02 * nrm(ks[24], (L, D), f32),
        "ln2_b": 0.02 * nrm(ks[25], (L, D), f32),
    }


def reference(x, positions, w_in, gate_b, w_branch, w_out, pool_w, pool_scale,
              rg_conv_w, rg_conv_b, rg_wa, rg_ba, rg_wx, rg_bx, rg_lambda,
              ln1_g, ln1_b, router_w, router_bias, exp_w1, exp_w3, exp_w2,
              sh_w1, sh_w3, sh_w2, ln2_g, ln2_b):
    B, S, D = x.shape
    cos, sin = rope_tables(positions)
    for l in range(DEPTH):
        m = token_mixer(x, cos, sin, w_in[l], gate_b[l], w_branch[l], w_out[l], pool_w[l], pool_scale[l],
                        rg_conv_w[l], rg_conv_b[l], rg_wa[l], rg_ba[l], rg_wx[l], rg_bx[l], rg_lambda[l])
        x = layer_norm(ALPHA * x + m, ln1_g[l], ln1_b[l])
        f = moe_ffn(x.reshape(B * S, D), router_w[l], router_bias[l], exp_w1[l], exp_w3[l], exp_w2[l],
                    sh_w1[l], sh_w3[l], sh_w2[l]).reshape(B, S, D)
        x = layer_norm(ALPHA * x + f, ln2_g[l], ln2_b[l])
    return x
```

```python
import functools
import math

import jax
import jax.numpy as jnp
from jax import lax
from jax.experimental import pallas as pl
from jax.experimental.pallas import tpu as pltpu

F32, BF16, I32 = jnp.float32, jnp.bfloat16, jnp.int32

RET_HEADS, RET_DH, RET_W, CHUNK = 4, 128, 512, 128
ROPE_BASE = 10000.0
POOL_WINDOWS, POOL_GW, POOL_W = (2, 4, 8, 16), 128, 512
RG_BLOCKS, RG_BD, RG_W, RG_CONV, RG_C = 4, 128, 512, 4, 8.0
N_BRANCH = 3
N_GROUPS, E_PER_GROUP, TOPK_GROUPS, TOP_K, D_EXPERT = 8, 8, 4, 8, 256
ROUTED_SCALE = 2.5
LN_EPS, GN_EPS = 1e-5, 1e-6

VMEM_LIMIT = 56 * 1024 * 1024
MM_TM, MM_TN = 1024, 512
BR_TS = 512
MG_TM = 512
SUB = 256
ROW_ALIGN = 16
FFN_BLK = 512
NEG_INF = float("-inf")


def _cparams(n_axes):
    return pltpu.CompilerParams(dimension_semantics=("arbitrary",) * n_axes, vmem_limit_bytes=VMEM_LIMIT)


def _layer_norm(z, g, b):
    mu = jnp.mean(z, axis=-1, keepdims=True)
    zc = z - mu
    var = jnp.mean(zc * zc, axis=-1, keepdims=True)
    return zc * lax.rsqrt(var + LN_EPS) * g + b


def _rope_kernel(pos_ref, inv_ref, sign_ref, cos_ref, sin_ref):
    ang = pos_ref[...].astype(F32) * inv_ref[...]
    cos_ref[...] = jnp.cos(ang)
    sin_ref[...] = jnp.sin(ang) * sign_ref[...]


def _rope_tables(positions):
    S = positions.shape[0]
    half = RET_DH // 2
    inv = ROPE_BASE ** (-jnp.arange(half, dtype=F32) / half)
    inv2 = jnp.concatenate([inv, inv]).reshape(1, RET_DH)
    sign = jnp.concatenate([-jnp.ones((half,), F32), jnp.ones((half,), F32)]).reshape(1, RET_DH)
    return pl.pallas_call(
        _rope_kernel,
        out_shape=(jax.ShapeDtypeStruct((S, RET_DH), F32), jax.ShapeDtypeStruct((S, RET_DH), F32)),
        name="rope_tables",
    )(positions.reshape(S, 1), inv2, sign)


def _mm_kernel(x_ref, w_ref, o_ref):
    o_ref[...] = jnp.dot(x_ref[...], w_ref[...], preferred_element_type=F32).astype(o_ref.dtype)


def _matmul(x, w, name):
    M, K = x.shape
    N = w.shape[1]
    tm, tn = min(MM_TM, M), MM_TN
    return pl.pallas_call(
        _mm_kernel,
        out_shape=jax.ShapeDtypeStruct((M, N), BF16),
        grid=(M // tm, N // tn),
        in_specs=[pl.BlockSpec((tm, K), lambda i, j: (i, 0)), pl.BlockSpec((K, tn), lambda i, j: (0, j))],
        out_specs=pl.BlockSpec((tm, tn), lambda i, j: (i, j)),
        compiler_params=_cparams(2),
        name=name,
    )(x, w)


def _log_sigmoid(x):
    return jnp.minimum(x, 0.0) - jnp.log(1.0 + jnp.exp(-jnp.abs(x)))


def _branch_kernel(ha_ref, cos_ref, sin_ref, dmask_ref, kdec_ref, qdec_ref, poolw_ref, pscale_ref,
                   convw_ref, convb_ref, wa_ref, ba_ref, wx_ref, bx_ref, lam_ref,
                   o_ref, state_ref, hcar_ref, pbuf_ref, cbuf_ref, *, ts, cdec):
    s = pl.program_id(1)

    @pl.when(s == 0)
    def _():
        state_ref[...] = jnp.zeros_like(state_ref)
        hcar_ref[...] = jnp.zeros_like(hcar_ref)
        pbuf_ref[0:16, :] = jnp.zeros((16, POOL_W), F32)
        cbuf_ref[0:8, :] = jnp.zeros((8, RG_W), F32)

    log_sig_lam = _log_sigmoid(lam_ref[...])

    def chunk(c, carry):
        r0 = pl.multiple_of(c * CHUNK, CHUNK)
        rows = pl.ds(r0, CHUNK)
        cosf = cos_ref[rows, :]
        sinf = sin_ref[rows, :]

        for h in range(RET_HEADS):
            cs = slice(h * RET_DH, (h + 1) * RET_DH)
            q = ha_ref[rows, cs].astype(F32)
            k = ha_ref[rows, slice(RET_W + h * RET_DH, RET_W + (h + 1) * RET_DH)].astype(F32)
            v = ha_ref[rows, slice(2 * RET_W + h * RET_DH, 2 * RET_W + (h + 1) * RET_DH)]
            g = ha_ref[rows, slice(3 * RET_W + h * RET_DH, 3 * RET_W + (h + 1) * RET_DH)].astype(F32)
            qr = q * cosf + pltpu.roll(q, RET_DH // 2, 1) * sinf
            kr = (k * cosf + pltpu.roll(k, RET_DH // 2, 1) * sinf) * (RET_DH ** -0.5)
            qb = qr.astype(BF16)
            sc = lax.dot_general(qb, kr.astype(BF16), (((1,), (1,)), ((), ())),
                                 preferred_element_type=F32) * dmask_ref[h]
            o = jnp.dot(sc.astype(BF16), v, preferred_element_type=F32)
            st = state_ref[h]
            o = o + jnp.dot(qb, st.astype(BF16), preferred_element_type=F32) * qdec_ref[h]
            kd = (kr * kdec_ref[h]).astype(BF16)
            state_ref[h] = st * cdec[h] + lax.dot_general(kd, v, (((0,), (0,)), ((), ())),
                                                          preferred_element_type=F32)
            mu = jnp.mean(o, axis=-1, keepdims=True)
            oc = o - mu
            var = jnp.mean(oc * oc, axis=-1, keepdims=True)
            o_ref[rows, cs] = (jax.nn.silu(g) * (oc * lax.rsqrt(var + GN_EPS))).astype(BF16)

        xp = ha_ref[rows, slice(4 * RET_W, 4 * RET_W + POOL_W)].astype(F32)
        pbuf_ref[16:16 + CHUNK, :] = xp
        t_idx = s * ts + r0 + lax.broadcasted_iota(I32, (CHUNK, POOL_GW), 0)
        for gi, w in enumerate(POOL_WINDOWS):
            cs = slice(gi * POOL_GW, (gi + 1) * POOL_GW)
            xg = xp[:, cs]
            win = xg
            for d in range(1, w):
                win = win + pbuf_ref[16 - d:16 - d + CHUNK, cs]
            n = jnp.minimum(w, t_idx + 1).astype(F32)
            pooled = win / n - xg
            y = jnp.dot(pooled.astype(BF16), poolw_ref[gi], preferred_element_type=F32) * pscale_ref[:, cs]
            o_ref[rows, slice(RET_W + gi * POOL_GW, RET_W + (gi + 1) * POOL_GW)] = y.astype(BF16)
        pbuf_ref[0:16, :] = pbuf_ref[CHUNK:CHUNK + 16, :]

        x0 = 4 * RET_W + POOL_W
        xr = ha_ref[rows, slice(x0, x0 + RG_W)].astype(F32)
        yr = ha_ref[rows, slice(x0 + RG_W, x0 + 2 * RG_W)].astype(F32)
        cbuf_ref[8:8 + CHUNK, :] = xr
        conv = convb_ref[...] + xr * convw_ref[0:1, :]
        for kk in range(1, RG_CONV):
            conv = conv + cbuf_ref[8 - kk:8 - kk + CHUNK, :] * convw_ref[kk:kk + 1, :]
        cbuf_ref[0:8, :] = cbuf_ref[CHUNK:CHUNK + 8, :]
        cb = conv.astype(BF16)
        rl = jnp.concatenate([jnp.dot(cb[:, n * RG_BD:(n + 1) * RG_BD], wa_ref[n], preferred_element_type=F32)
                              for n in range(RG_BLOCKS)], axis=1)
        il = jnp.concatenate([jnp.dot(cb[:, n * RG_BD:(n + 1) * RG_BD], wx_ref[n], preferred_element_type=F32)
                              for n in range(RG_BLOCKS)], axis=1)
        r = jax.nn.sigmoid(rl + ba_ref[...])
        ig = jax.nn.sigmoid(il + bx_ref[...])
        log_a = RG_C * r * log_sig_lam
        a = jnp.exp(log_a)
        bb = jnp.sqrt(-jnp.tanh(log_a) * (a * a + 1.0)) * (ig * conv)
        row = lax.broadcasted_iota(I32, (CHUNK, RG_W), 0)
        A, Bv = a, bb
        d = 1
        while d < CHUNK:
            As = jnp.where(row < d, 1.0, pltpu.roll(A, d, 0))
            Bs = jnp.where(row < d, 0.0, pltpu.roll(Bv, d, 0))
            Bv = A * Bs + Bv
            A = A * As
            d *= 2
        hh = A * hcar_ref[...] + Bv
        hcar_ref[...] = hh[CHUNK - 1:CHUNK, :]
        o_ref[rows, slice(RET_W + POOL_W, RET_W + POOL_W + RG_W)] = (hh * jax.nn.gelu(yr)).astype(BF16)
        return carry

    lax.fori_loop(0, ts // CHUNK, chunk, 0)


def _retention_consts():
    log_g = [math.log(1.0 - 2.0 ** (-5.0 - h)) for h in range(RET_HEADS)]
    lg = jnp.asarray(log_g, F32)
    idx = jnp.arange(CHUNK, dtype=F32)
    diff = idx[:, None] - idx[None, :]
    dmask = jnp.where(diff >= 0, jnp.exp(lg[:, None, None] * jnp.maximum(diff, 0.0)), 0.0)
    kdec = jnp.exp(lg[:, None] * (CHUNK - 1 - idx)[None, :])
    qdec = jnp.exp(lg[:, None] * (idx + 1.0)[None, :])
    kdec = jnp.broadcast_to(kdec[:, :, None], (RET_HEADS, CHUNK, RET_DH))
    qdec = jnp.broadcast_to(qdec[:, :, None], (RET_HEADS, CHUNK, RET_DH))
    cdec = tuple(math.exp(v * CHUNK) for v in log_g)
    return dmask, kdec, qdec, cdec


def _branches(ha, cos, sin, consts, pool_w, pool_scale, conv_w, conv_b, wa, ba, wx, bx, lam, B, S):
    dmask, kdec, qdec, cdec = consts
    ts = min(BR_TS, S)
    nst = S // ts
    HA = ha.shape[1]
    OW = RET_W + POOL_W + RG_W
    full = lambda shape: pl.BlockSpec(shape, lambda b, s: (0,) * len(shape))
    return pl.pallas_call(
        functools.partial(_branch_kernel, ts=ts, cdec=cdec),
        out_shape=jax.ShapeDtypeStruct((B * S, OW), BF16),
        grid=(B, nst),
        in_specs=[
            pl.BlockSpec((ts, HA), lambda b, s: (b * nst + s, 0)),
            pl.BlockSpec((ts, RET_DH), lambda b, s: (s, 0)),
            pl.BlockSpec((ts, RET_DH), lambda b, s: (s, 0)),
            full((RET_HEADS, CHUNK, CHUNK)), full((RET_HEADS, CHUNK, RET_DH)), full((RET_HEADS, CHUNK, RET_DH)),
            full((len(POOL_WINDOWS), POOL_GW, POOL_GW)), full((1, POOL_W)),
            full((RG_CONV, RG_W)), full((1, RG_W)),
            full((RG_BLOCKS, RG_BD, RG_BD)), full((1, RG_W)),
            full((RG_BLOCKS, RG_BD, RG_BD)), full((1, RG_W)), full((1, RG_W)),
        ],
        out_specs=pl.BlockSpec((ts, OW), lambda b, s: (b * nst + s, 0)),
        scratch_shapes=[
            pltpu.VMEM((RET_HEADS, RET_DH, RET_DH), F32),
            pltpu.VMEM((1, RG_W), F32),
            pltpu.VMEM((16 + CHUNK, POOL_W), F32),
            pltpu.VMEM((8 + CHUNK, RG_W), F32),
        ],
        compiler_params=_cparams(2),
        name="mixer_branches",
    )(ha, cos, sin, dmask, kdec, qdec, pool_w, pool_scale, conv_w, conv_b, wa, ba, wx, bx, lam)


def _merge_kernel(br_ref, gl_ref, x_ref, wbr_ref, wout_ref, gb_ref, lng_ref, lnb_ref, o32_ref, o16_ref, *, alpha):
    D = x_ref.shape[1]
    acc = None
    for i in range(N_BRANCH):
        y = jnp.dot(br_ref[:, i * RET_W:(i + 1) * RET_W], wbr_ref[i], preferred_element_type=F32)
        gate = jax.nn.sigmoid(gl_ref[:, i * D:(i + 1) * D].astype(F32) + gb_ref[i:i + 1, :])
        acc = gate * y if acc is None else acc + gate * y
    m = jnp.dot(acc.astype(BF16), wout_ref[...], preferred_element_type=F32)
    z = _layer_norm(alpha * x_ref[...] + m, lng_ref[...], lnb_ref[...])
    o32_ref[...] = z
    o16_ref[...] = z.astype(BF16)


def _merge(br, gl, x32, w_branch, w_out, gate_b, ln_g, ln_b, alpha):
    T, D = x32.shape
    tm = min(MG_TM, T)
    full = lambda shape: pl.BlockSpec(shape, lambda i: (0,) * len(shape))
    return pl.pallas_call(
        functools.partial(_merge_kernel, alpha=alpha),
        out_shape=(jax.ShapeDtypeStruct((T, D), F32), jax.ShapeDtypeStruct((T, D), BF16)),
        grid=(T // tm,),
        in_specs=[
            pl.BlockSpec((tm, br.shape[1]), lambda i: (i, 0)),
            pl.BlockSpec((tm, gl.shape[1]), lambda i: (i, 0)),
            pl.BlockSpec((tm, D), lambda i: (i, 0)),
            full(w_branch.shape), full(w_out.shape), full(gate_b.shape), full((1, D)), full((1, D)),
        ],
        out_specs=(pl.BlockSpec((tm, D), lambda i: (i, 0)), pl.BlockSpec((tm, D), lambda i: (i, 0))),
        compiler_params=_cparams(1),
        name="mixer_merge",
    )(br, gl, x32, w_branch, w_out, gate_b, ln_g, ln_b)


def _first_argmax(v, iota, n):
    m = jnp.max(v, axis=0, keepdims=True)
    idx = jnp.min(jnp.where(v == m, iota, n), axis=0, keepdims=True)
    return m, idx


def _router_kernel(x_ref, rwt_ref, rb_ref, tri_ref, e_ref, g_ref, rk_ref, cnt_ref):
    E = rwt_ref.shape[0]
    nt = x_ref.shape[0]
    logits = lax.dot_general(rwt_ref[...], x_ref[...], (((1,), (1,)), ((), ())),
                             preferred_element_type=F32, precision=lax.Precision.HIGHEST)
    scores = jax.nn.sigmoid(logits)
    biased = scores + rb_ref[...]

    sub = lax.broadcasted_iota(I32, (E_PER_GROUP, nt), 0)
    giota = lax.broadcasted_iota(I32, (N_GROUPS, nt), 0)
    gs = jnp.zeros((N_GROUPS, nt), F32)
    for g in range(N_GROUPS):
        bg = biased[g * E_PER_GROUP:(g + 1) * E_PER_GROUP, :]
        m1, i1 = _first_argmax(bg, sub, E_PER_GROUP)
        m2 = jnp.max(jnp.where(sub == i1, NEG_INF, bg), axis=0, keepdims=True)
        gs = jnp.where(giota == g, m1 + m2, gs)
    gsel = jnp.zeros((N_GROUPS, nt), jnp.bool_)
    v = gs
    for _ in range(TOPK_GROUPS):
        _, idx = _first_argmax(v, giota, N_GROUPS)
        hit = giota == idx
        gsel = jnp.logical_or(gsel, hit)
        v = jnp.where(hit, NEG_INF, v)
    gself = jnp.where(gsel, 1.0, 0.0)
    masked = jnp.concatenate(
        [jnp.where(jnp.broadcast_to(gself[g:g + 1, :], (E_PER_GROUP, nt)) > 0.5,
                   biased[g * E_PER_GROUP:(g + 1) * E_PER_GROUP, :], NEG_INF) for g in range(N_GROUPS)], axis=0)

    eiota = lax.broadcasted_iota(I32, (E, nt), 0)
    v = masked
    onehot = jnp.zeros((E, nt), F32)
    idxs, sels = [], []
    for _ in range(TOP_K):
        _, idx = _first_argmax(v, eiota, E)
        hit = eiota == idx
        sels.append(jnp.sum(jnp.where(hit, scores, 0.0), axis=0, keepdims=True))
        idxs.append(idx)
        onehot = onehot + jnp.where(hit, 1.0, 0.0)
        v = jnp.where(hit, NEG_INF, v)
    ssum = sels[0]
    for k in range(1, TOP_K):
        ssum = ssum + sels[k]
    excl = jnp.dot(onehot.astype(BF16), tri_ref[...], preferred_element_type=F32)
    for k in range(TOP_K):
        e_ref[k:k + 1, :] = idxs[k]
        g_ref[k:k + 1, :] = sels[k] / ssum * ROUTED_SCALE
        rk_ref[k:k + 1, :] = jnp.sum(jnp.where(eiota == idxs[k], excl, 0.0), axis=0, keepdims=True).astype(I32)
    cnt_ref[...] = jnp.sum(onehot, axis=1, keepdims=True).astype(I32)


def _router(x32, rwt, rb, tri):
    T, D = x32.shape
    E = rwt.shape[0]
    J = T // SUB
    row = lambda dt: jax.ShapeDtypeStruct((TOP_K, T), dt)
    return pl.pallas_call(
        _router_kernel,
        out_shape=(row(I32), row(F32), row(I32), jax.ShapeDtypeStruct((J, E, 1), I32)),
        grid=(J,),
        in_specs=[
            pl.BlockSpec((SUB, D), lambda j: (j, 0)),
            pl.BlockSpec((E, D), lambda j: (0, 0)),
            pl.BlockSpec((E, 1), lambda j: (0, 0)),
            pl.BlockSpec((SUB, SUB), lambda j: (0, 0)),
        ],
        out_specs=(
            pl.BlockSpec((TOP_K, SUB), lambda j: (0, j)),
            pl.BlockSpec((TOP_K, SUB), lambda j: (0, j)),
            pl.BlockSpec((TOP_K, SUB), lambda j: (0, j)),
            pl.BlockSpec((None, E, 1), lambda j: (j, 0, 0)),
        ),
        compiler_params=_cparams(1),
        name="moe_router",
    )(x32, rwt, rb, tri)


def _local_rows_max(E):
    lmax = SUB * TOP_K + E * (ROW_ALIGN - 1)
    return -(-lmax // SUB) * SUB


def _num_blocks_max(T, E):
    J = T // SUB
    rows = T * TOP_K + J * E * (ROW_ALIGN - 1) + E * (FFN_BLK - ROW_ALIGN)
    return -(-rows // FFN_BLK)


def _tables(cnt, nblk_max):
    J, E = cnt.shape
    pc = (cnt + ROW_ALIGN - 1) // ROW_ALIGN * ROW_ALIGN
    lo = jnp.cumsum(pc, axis=1) - pc
    tot = jnp.sum(pc, axis=1)
    se = jnp.sum(pc, axis=0)
    reg = (se + FFN_BLK - 1) // FFN_BLK * FFN_BLK
    creg = jnp.cumsum(reg)
    gs = creg - reg
    go = gs[None, :] + jnp.cumsum(pc, axis=0) - pc
    nused = (creg[-1] // FFN_BLK).reshape(1)
    bstart = jnp.arange(nblk_max, dtype=I32) * FFN_BLK
    bexp = jnp.minimum(jnp.searchsorted(creg, bstart, side="right"), E - 1)
    i32 = lambda a: a.astype(I32)
    return dict(lo=i32(lo), go=i32(go), pc=i32(pc), tot=i32(tot), tail_start=i32(gs + se), tail_len=i32(reg - se),
                nused=i32(nused), bexp=i32(bexp))


def _wait_rows(src_ref, dst_ref, sem, nrows, big):
    def wbig(i, c):
        pltpu.make_async_copy(src_ref.at[pl.ds(0, big)], dst_ref.at[pl.ds(0, big)], sem).wait()
        return c

    def wsmall(i, c):
        pltpu.make_async_copy(src_ref.at[pl.ds(0, ROW_ALIGN)], dst_ref.at[pl.ds(0, ROW_ALIGN)], sem).wait()
        return c

    lax.fori_loop(0, nrows // big, wbig, 0)
    lax.fori_loop(0, (nrows % big) // ROW_ALIGN, wsmall, 0)


def _dispatch_kernel(lo_s, go_s, pc_s, tot_s, tls_s, tll_s,
                     e_ref, rk_ref, lov_ref, x_ref, xs_ref, slot_ref, xbuf, zbuf, sem, *, E, J):
    j = pl.program_id(0)
    par = j % 2
    nt = x_ref.shape[0]

    eiota = lax.broadcasted_iota(I32, (E, nt), 0)
    for k in range(TOP_K):
        base = jnp.sum(jnp.where(eiota == e_ref[k:k + 1, :], lov_ref[...], 0), axis=0, keepdims=True)
        slot_ref[k:k + 1, :] = base + rk_ref[k:k + 1, :]
    slots = slot_ref[...]
    xb = x_ref[...]
    tot = tot_s[j]
    buf = xbuf.at[par]

    def chunk(c, carry):
        base = pl.multiple_of(c * SUB, SUB)
        siota = lax.broadcasted_iota(I32, (SUB, nt), 0) + base
        p = siota == slots[0:1, :]
        for k in range(1, TOP_K):
            p = jnp.logical_or(p, siota == slots[k:k + 1, :])
        pb = jnp.where(p, 1.0, 0.0).astype(BF16)
        buf[pl.ds(base, SUB), :] = jnp.dot(pb, xb, preferred_element_type=F32).astype(BF16)
        return carry

    lax.fori_loop(0, (tot + SUB - 1) // SUB, chunk, 0)

    @pl.when(j > 0)
    def _():
        _wait_rows(xbuf.at[1 - par], xs_ref, sem.at[1 - par], tot_s[j - 1], SUB)

    def per_expert(e, carry):
        src0 = lo_s[j * E + e]
        dst0 = go_s[j * E + e]

        def per_tile(i, c):
            so = pl.multiple_of(src0 + i * ROW_ALIGN, ROW_ALIGN)
            do = pl.multiple_of(dst0 + i * ROW_ALIGN, ROW_ALIGN)
            pltpu.make_async_copy(buf.at[pl.ds(so, ROW_ALIGN)], xs_ref.at[pl.ds(do, ROW_ALIGN)], sem.at[par]).start()
            return c

        lax.fori_loop(0, pc_s[j * E + e] // ROW_ALIGN, per_tile, 0)
        return carry

    lax.fori_loop(0, E, per_expert, 0)

    @pl.when(j == 0)
    def _():
        zbuf[...] = jnp.zeros_like(zbuf)

        def per_expert_tail(e, ntile):
            n = tll_s[e] // ROW_ALIGN

            def per_tile(i, c):
                do = pl.multiple_of(tls_s[e] + i * ROW_ALIGN, ROW_ALIGN)
                pltpu.make_async_copy(zbuf, xs_ref.at[pl.ds(do, ROW_ALIGN)], sem.at[2]).start()
                return c

            lax.fori_loop(0, n, per_tile, 0)
            return ntile + n

        ntile = lax.fori_loop(0, E, per_expert_tail, 0)

        def wtail(i, c):
            pltpu.make_async_copy(zbuf, xs_ref.at[pl.ds(0, ROW_ALIGN)], sem.at[2]).wait()
            return c

        lax.fori_loop(0, ntile, wtail, 0)

    @pl.when(j == J - 1)
    def _():
        _wait_rows(xbuf.at[par], xs_ref, sem.at[par], tot, SUB)


def _dispatch(tb, top_e, rank, x16, nblk_max):
    T, D = x16.shape
    J, E = tb["lo"].shape
    lmax = _local_rows_max(E)
    gs = pltpu.PrefetchScalarGridSpec(
        num_scalar_prefetch=6,
        grid=(J,),
        in_specs=[
            pl.BlockSpec((TOP_K, SUB), lambda j, *_: (0, j)),
            pl.BlockSpec((TOP_K, SUB), lambda j, *_: (0, j)),
            pl.BlockSpec((None, E, 1), lambda j, *_: (j, 0, 0)),
            pl.BlockSpec((SUB, D), lambda j, *_: (j, 0)),
        ],
        out_specs=(pl.BlockSpec(memory_space=pl.ANY), pl.BlockSpec((TOP_K, SUB), lambda j, *_: (0, j))),
        scratch_shapes=[
            pltpu.VMEM((2, lmax, D), BF16),
            pltpu.VMEM((ROW_ALIGN, D), BF16),
            pltpu.SemaphoreType.DMA((3,)),
        ],
    )
    return pl.pallas_call(
        functools.partial(_dispatch_kernel, E=E, J=J),
        out_shape=(jax.ShapeDtypeStruct((nblk_max * FFN_BLK, D), BF16), jax.ShapeDtypeStruct((TOP_K, T), I32)),
        grid_spec=gs,
        compiler_params=_cparams(1),
        name="moe_dispatch",
    )(tb["lo"].reshape(-1), tb["go"].reshape(-1), tb["pc"].reshape(-1), tb["tot"], tb["tail_start"], tb["tail_len"],
      top_e, rank, tb["lo"].reshape(J, E, 1), x16)


def _ffn_kernel(be_s, nu_s, x_ref, w13_ref, w2_ref, y_ref):
    @pl.when(pl.program_id(0) < nu_s[0])
    def _():
        h = jnp.dot(x_ref[...], w13_ref[...], preferred_element_type=F32)
        hh = (jax.nn.silu(h[:, :D_EXPERT]) * h[:, D_EXPERT:]).astype(BF16)
        y_ref[...] = jnp.dot(hh, w2_ref[...], preferred_element_type=F32).astype(BF16)


def _expert_ffn(tb, xs, w13, w2, nblk_max):
    D = xs.shape[1]
    blk = lambda b, be, nu: jnp.minimum(b, nu[0] - 1)
    gs = pltpu.PrefetchScalarGridSpec(
        num_scalar_prefetch=2,
        grid=(nblk_max,),
        in_specs=[
            pl.BlockSpec((FFN_BLK, D), lambda b, be, nu: (blk(b, be, nu), 0)),
            pl.BlockSpec((None, D, 2 * D_EXPERT), lambda b, be, nu: (be[blk(b, be, nu)], 0, 0)),
            pl.BlockSpec((None, D_EXPERT, D), lambda b, be, nu: (be[blk(b, be, nu)], 0, 0)),
        ],
        out_specs=pl.BlockSpec((FFN_BLK, D), lambda b, be, nu: (blk(b, be, nu), 0)),
    )
    return pl.pallas_call(
        _ffn_kernel,
        out_shape=jax.ShapeDtypeStruct(xs.shape, BF16),
        grid_spec=gs,
        compiler_params=_cparams(1),
        name="moe_expert_ffn",
    )(tb["bexp"], tb["nused"], xs, w13, w2)


def _combine_kernel(lo_s, go_s, pc_s, tot_s,
                    slot_ref, g_ref, x32_ref, x16_ref, ys_ref, sw13_ref, sw2_ref, lng_ref, lnb_ref,
                    o32_ref, o16_ref, ybuf, acc_ref, sem, *, E, alpha):
    j = pl.program_id(0)
    nt = x32_ref.shape[0]
    tot = tot_s[j]

    def per_expert(e, carry):
        src0 = go_s[j * E + e]
        dst0 = lo_s[j * E + e]

        def per_tile(i, c):
            so = pl.multiple_of(src0 + i * ROW_ALIGN, ROW_ALIGN)
            do = pl.multiple_of(dst0 + i * ROW_ALIGN, ROW_ALIGN)
            pltpu.make_async_copy(ys_ref.at[pl.ds(so, ROW_ALIGN)], ybuf.at[pl.ds(do, ROW_ALIGN)], sem.at[0]).start()
            return c

        lax.fori_loop(0, pc_s[j * E + e] // ROW_ALIGN, per_tile, 0)
        return carry

    lax.fori_loop(0, E, per_expert, 0)

    nch = (tot + SUB - 1) // SUB

    def ztile(i, c):
        zo = pl.multiple_of(tot + i * ROW_ALIGN, ROW_ALIGN)
        ybuf[pl.ds(zo, ROW_ALIGN), :] = jnp.zeros((ROW_ALIGN, ybuf.shape[1]), BF16)
        return c

    lax.fori_loop(0, (nch * SUB - tot) // ROW_ALIGN, ztile, 0)

    xb = x16_ref[...]
    h = jnp.dot(xb, sw13_ref[...], preferred_element_type=F32)
    hh = (jax.nn.silu(h[:, :D_EXPERT]) * h[:, D_EXPERT:]).astype(BF16)
    acc_ref[...] = jnp.dot(hh, sw2_ref[...], preferred_element_type=F32)

    _wait_rows(ys_ref, ybuf, sem.at[0], tot, SUB)

    slots = slot_ref[...]
    gates = g_ref[...]

    def chunk(c, carry):
        base = pl.multiple_of(c * SUB, SUB)
        siota = lax.broadcasted_iota(I32, (SUB, nt), 0) + base
        wt = jnp.where(siota == slots[0:1, :], gates[0:1, :], 0.0)
        for k in range(1, TOP_K):
            wt = wt + jnp.where(siota == slots[k:k + 1, :], gates[k:k + 1, :], 0.0)
        acc_ref[...] += lax.dot_general(wt.astype(BF16), ybuf[pl.ds(base, SUB), :], (((0,), (0,)), ((), ())),
                                        preferred_element_type=F32)
        return carry

    lax.fori_loop(0, nch, chunk, 0)

    z = _layer_norm(alpha * x32_ref[...] + acc_ref[...], lng_ref[...], lnb_ref[...])
    o32_ref[...] = z
    o16_ref[...] = z.astype(BF16)


def _combine(tb, slot, gate, x32, x16, ys, sw13, sw2, ln_g, ln_b, alpha):
    T, D = x32.shape
    J, E = tb["lo"].shape
    lmax = _local_rows_max(E)
    full = lambda shape: pl.BlockSpec(shape, lambda j, *_: (0,) * len(shape))
    gs = pltpu.PrefetchScalarGridSpec(
        num_scalar_prefetch=4,
        grid=(J,),
        in_specs=[
            pl.BlockSpec((TOP_K, SUB), lambda j, *_: (0, j)),
            pl.BlockSpec((TOP_K, SUB), lambda j, *_: (0, j)),
            pl.BlockSpec((SUB, D), lambda j, *_: (j, 0)),
            pl.BlockSpec((SUB, D), lambda j, *_: (j, 0)),
            pl.BlockSpec(memory_space=pl.ANY),
            full(sw13.shape), full(sw2.shape), full((1, D)), full((1, D)),
        ],
        out_specs=(pl.BlockSpec((SUB, D), lambda j, *_: (j, 0)), pl.BlockSpec((SUB, D), lambda j, *_: (j, 0))),
        scratch_shapes=[
            pltpu.VMEM((lmax, D), BF16),
            pltpu.VMEM((SUB, D), F32),
            pltpu.SemaphoreType.DMA((1,)),
        ],
    )
    return pl.pallas_call(
        functools.partial(_combine_kernel, E=E, alpha=alpha),
        out_shape=(jax.ShapeDtypeStruct((T, D), F32), jax.ShapeDtypeStruct((T, D), BF16)),
        grid_spec=gs,
        compiler_params=_cparams(1),
        name="moe_combine",
    )(tb["lo"].reshape(-1), tb["go"].reshape(-1), tb["pc"].reshape(-1), tb["tot"],
      slot, gate, x32, x16, ys, sw13, sw2, ln_g, ln_b)


def kernel(x, positions, w_in, gate_b, w_branch, w_out, pool_w, pool_scale, rg_conv_w, rg_conv_b, rg_wa, rg_ba,
           rg_wx, rg_bx, rg_lambda, ln1_g, ln1_b, router_w, router_bias, exp_w1, exp_w3, exp_w2, sh_w1, sh_w3,
           sh_w2, ln2_g, ln2_b):
    B, S, D = x.shape
    L = w_in.shape[0]
    T = B * S
    E = router_w.shape[2]
    assert T % SUB == 0 and S % CHUNK == 0 and E == N_GROUPS * E_PER_GROUP
    alpha = (2 * L) ** 0.25
    n_a = 4 * RET_W + POOL_W + 2 * RG_W
    nblk_max = _num_blocks_max(T, E)

    cos, sin = _rope_tables(positions)
    consts = _retention_consts()
    tri = (jnp.arange(SUB)[:, None] < jnp.arange(SUB)[None, :]).astype(BF16)
    row = lambda a: a.reshape(1, -1)

    x32 = x.reshape(T, D)
    x16 = x32.astype(BF16)
    for l in range(L):
        w_in16 = w_in[l].astype(BF16)
        ha = _matmul(x16, w_in16[:, :n_a], "in_proj_a")
        gl = _matmul(x16, w_in16[:, n_a:], "in_proj_gates")
        br = _branches(ha, cos, sin, consts, pool_w[l].astype(BF16), row(pool_scale[l]), rg_conv_w[l],
                       row(rg_conv_b[l]), rg_wa[l].astype(BF16), row(rg_ba[l]), rg_wx[l].astype(BF16),
                       row(rg_bx[l]), row(rg_lambda[l]), B, S)
        x32, x16 = _merge(br, gl, x32, w_branch[l].astype(BF16), w_out[l].astype(BF16), gate_b[l],
                          row(ln1_g[l]), row(ln1_b[l]), alpha)

        top_e, gate, rank, cnt = _router(x32, router_w[l].T, router_bias[l].reshape(E, 1), tri)
        tb = _tables(cnt[:, :, 0], nblk_max)
        xs, slot = _dispatch(tb, top_e, rank, x16, nblk_max)
        w13 = jnp.concatenate([exp_w1[l], exp_w3[l]], axis=-1).astype(BF16)
        ys = _expert_ffn(tb, xs, w13, exp_w2[l].astype(BF16), nblk_max)
        sw13 = jnp.concatenate([sh_w1[l], sh_w3[l]], axis=-1).astype(BF16)
        x32, x16 = _combine(tb, slot, gate, x32, x16, ys, sw13, sh_w2[l].astype(BF16),
                            row(ln2_g[l]), row(ln2_b[l]), alpha)
    return x32.reshape(B, S, D)
```

```python
import functools
import math

import jax
import jax.numpy as jnp
from jax import lax
from jax.experimental import pallas as pl
from jax.experimental.pallas import tpu as pltpu

F32, BF16, I32 = jnp.float32, jnp.bfloat16, jnp.int32

RET_HEADS, RET_DH, RET_W, CHUNK = 4, 128, 512, 128
ROPE_BASE = 10000.0
POOL_WINDOWS, POOL_GW, POOL_W = (2, 4, 8, 16), 128, 512
RG_BLOCKS, RG_BD, RG_W, RG_CONV, RG_C = 4, 128, 512, 4, 8.0
HIST = 16
N_BRANCH = 3
N_GROUPS, E_PER_GROUP, TOPK_GROUPS, TOP_K, D_EXPERT = 8, 8, 4, 8, 256
ROUTED_SCALE = 2.5
LN_EPS, GN_EPS = 1e-5, 1e-6

VMEM_LIMIT = 56 * 1024 * 1024
MM_TM, MM_TN = 2048, 512
BR_TS = 512
MG_TM = 512
SUB = 256
ROW_ALIGN = 16
ISSUE_UNROLL = 4
LOCAL_ALIGN = ROW_ALIGN * ISSUE_UNROLL
PAIR = 2 * SUB
FFN_BLK = 1024
NEG_INF = float("-inf")


def _cparams(n_axes):
    return pltpu.CompilerParams(dimension_semantics=("arbitrary",) * n_axes, vmem_limit_bytes=VMEM_LIMIT)


def _layer_norm(z, g, b):
    mu = jnp.mean(z, axis=-1, keepdims=True)
    zc = z - mu
    var = jnp.mean(zc * zc, axis=-1, keepdims=True)
    return zc * lax.rsqrt(var + LN_EPS) * g + b


def _rope_kernel(pos_ref, inv_ref, sign_ref, cos_ref, sin_ref):
    ang = pos_ref[...].astype(F32) * inv_ref[...]
    cos_ref[...] = jnp.cos(ang)
    sin_ref[...] = jnp.sin(ang) * sign_ref[...]


def _rope_tables(positions):
    S = positions.shape[0]
    half = RET_DH // 2
    inv = ROPE_BASE ** (-jnp.arange(half, dtype=F32) / half)
    inv2 = jnp.concatenate([inv, inv]).reshape(1, RET_DH)
    sign = jnp.concatenate([-jnp.ones((half,), F32), jnp.ones((half,), F32)]).reshape(1, RET_DH)
    return pl.pallas_call(
        _rope_kernel,
        out_shape=(jax.ShapeDtypeStruct((S, RET_DH), F32), jax.ShapeDtypeStruct((S, RET_DH), F32)),
        name="rope_tables",
    )(positions.reshape(S, 1), inv2, sign)


def _mm_kernel(x_ref, w_ref, o_ref):
    o_ref[...] = jnp.dot(x_ref[...], w_ref[...], preferred_element_type=F32).astype(o_ref.dtype)


def _matmul(x, w, name):
    M, K = x.shape
    N = w.shape[1]
    tm, tn = min(MM_TM, M), MM_TN
    return pl.pallas_call(
        _mm_kernel,
        out_shape=jax.ShapeDtypeStruct((M, N), BF16),
        grid=(M // tm, N // tn),
        in_specs=[pl.BlockSpec((tm, K), lambda i, j: (i, 0)), pl.BlockSpec((K, tn), lambda i, j: (0, j))],
        out_specs=pl.BlockSpec((tm, tn), lambda i, j: (i, j)),
        compiler_params=_cparams(2),
        name=name,
    )(x, w)


def _log_sigmoid(x):
    return jnp.minimum(x, 0.0) - jnp.log(1.0 + jnp.exp(-jnp.abs(x)))


def _branch_kernel(ha_ref, cos_ref, sin_ref, dmask_ref, kdec_ref, qdec_ref, band_ref, shift_ref, poolw_ref,
                   pscale_ref, convw_ref, convb_ref, wa_ref, ba_ref, wx_ref, bx_ref, lam_ref,
                   o_ref, state_ref, hcar_ref, hist_ref, *, ts, cdec):
    s = pl.program_id(1)
    x0 = 4 * RET_W

    @pl.when(s == 0)
    def _():
        state_ref[...] = jnp.zeros_like(state_ref)
        hcar_ref[...] = jnp.zeros_like(hcar_ref)
        hist_ref[...] = jnp.zeros_like(hist_ref)

    log_sig_lam = _log_sigmoid(lam_ref[...])

    def chunk(c, carry):
        r0 = pl.multiple_of(c * CHUNK, CHUNK)
        rows = pl.ds(r0, CHUNK)
        cosf = cos_ref[rows, :]
        sinf = sin_ref[rows, :]

        hist_ref[CHUNK:2 * CHUNK, :] = ha_ref[rows, slice(x0, x0 + POOL_W + RG_W)]
        wins = [jnp.dot(band_ref[gi], hist_ref[:, gi * POOL_GW:(gi + 1) * POOL_GW], preferred_element_type=F32)
                for gi in range(len(POOL_WINDOWS))]
        lagged = [jnp.dot(shift_ref[kk - 1], hist_ref[:, POOL_W:POOL_W + RG_W], preferred_element_type=F32)
                  for kk in range(1, RG_CONV)]
        hist_ref[CHUNK - HIST:CHUNK, :] = hist_ref[2 * CHUNK - HIST:2 * CHUNK, :]

        def retention_head(h):
            cs = slice(h * RET_DH, (h + 1) * RET_DH)
            q = ha_ref[rows, cs].astype(F32)
            k = ha_ref[rows, slice(RET_W + h * RET_DH, RET_W + (h + 1) * RET_DH)].astype(F32)
            v = ha_ref[rows, slice(2 * RET_W + h * RET_DH, 2 * RET_W + (h + 1) * RET_DH)]
            g = ha_ref[rows, slice(3 * RET_W + h * RET_DH, 3 * RET_W + (h + 1) * RET_DH)].astype(F32)
            qr = q * cosf + pltpu.roll(q, RET_DH // 2, 1) * sinf
            kr = (k * cosf + pltpu.roll(k, RET_DH // 2, 1) * sinf) * (RET_DH ** -0.5)
            qb = qr.astype(BF16)
            sc = lax.dot_general(qb, kr.astype(BF16), (((1,), (1,)), ((), ())),
                                 preferred_element_type=F32) * dmask_ref[h]
            o = jnp.dot(sc.astype(BF16), v, preferred_element_type=F32)
            st = state_ref[h]
            o = o + jnp.dot(qb, st.astype(BF16), preferred_element_type=F32) * qdec_ref[h]
            kd = (kr * kdec_ref[h]).astype(BF16)
            state_ref[h] = st * cdec[h] + lax.dot_general(kd, v, (((0,), (0,)), ((), ())),
                                                          preferred_element_type=F32)
            mu = jnp.mean(o, axis=-1, keepdims=True)
            oc = o - mu
            var = jnp.mean(oc * oc, axis=-1, keepdims=True)
            o_ref[rows, cs] = (jax.nn.silu(g) * (oc * lax.rsqrt(var + GN_EPS))).astype(BF16)

        retention_head(0)

        xr = ha_ref[rows, slice(x0 + POOL_W, x0 + POOL_W + RG_W)].astype(F32)
        conv = convb_ref[...] + xr * convw_ref[0:1, :]
        for kk in range(1, RG_CONV):
            conv = conv + lagged[kk - 1] * convw_ref[kk:kk + 1, :]
        cb = conv.astype(BF16)
        rl = jnp.concatenate([jnp.dot(cb[:, n * RG_BD:(n + 1) * RG_BD], wa_ref[n], preferred_element_type=F32)
                              for n in range(RG_BLOCKS)], axis=1)
        il = jnp.concatenate([jnp.dot(cb[:, n * RG_BD:(n + 1) * RG_BD], wx_ref[n], preferred_element_type=F32)
                              for n in range(RG_BLOCKS)], axis=1)

        retention_head(1)

        r = jax.nn.sigmoid(rl + ba_ref[...])
        ig = jax.nn.sigmoid(il + bx_ref[...])
        log_a = RG_C * r * log_sig_lam
        a = jnp.exp(log_a)
        om = -jnp.tanh(log_a) * (a * a + 1.0)
        bb = jnp.where(om > 0.0, om * lax.rsqrt(om), 0.0) * (ig * conv)

        retention_head(2)

        row = lax.broadcasted_iota(I32, (CHUNK, RG_W), 0)
        A, Bv = a, bb
        d = 1
        while d < CHUNK:
            As = jnp.where(row < d, 1.0, pltpu.roll(A, d, 0))
            Bs = jnp.where(row < d, 0.0, pltpu.roll(Bv, d, 0))
            Bv = A * Bs + Bv
            A = A * As
            d *= 2

        retention_head(3)

        hh = A * hcar_ref[...] + Bv
        hcar_ref[...] = hh[CHUNK - 1:CHUNK, :]
        yr = ha_ref[rows, slice(x0 + POOL_W + RG_W, x0 + POOL_W + 2 * RG_W)].astype(F32)
        o_ref[rows, slice(RET_W + POOL_W, RET_W + POOL_W + RG_W)] = (hh * jax.nn.gelu(yr)).astype(BF16)

        xp = ha_ref[rows, slice(x0, x0 + POOL_W)].astype(F32)
        t_idx = s * ts + r0 + lax.broadcasted_iota(I32, (CHUNK, POOL_GW), 0)
        for gi, w in enumerate(POOL_WINDOWS):
            cs = slice(gi * POOL_GW, (gi + 1) * POOL_GW)
            n = jnp.minimum(w, t_idx + 1).astype(F32)
            pooled = wins[gi] / n - xp[:, cs]
            y = jnp.dot(pooled.astype(BF16), poolw_ref[gi], preferred_element_type=F32) * pscale_ref[:, cs]
            o_ref[rows, slice(RET_W + gi * POOL_GW, RET_W + (gi + 1) * POOL_GW)] = y.astype(BF16)
        return carry

    lax.fori_loop(0, ts // CHUNK, chunk, 0, unroll=2)


def _retention_consts():
    log_g = [math.log(1.0 - 2.0 ** (-5.0 - h)) for h in range(RET_HEADS)]
    lg = jnp.asarray(log_g, F32)
    idx = jnp.arange(CHUNK, dtype=F32)
    diff = idx[:, None] - idx[None, :]
    dmask = jnp.where(diff >= 0, jnp.exp(lg[:, None, None] * jnp.maximum(diff, 0.0)), 0.0)
    kdec = jnp.exp(lg[:, None] * (CHUNK - 1 - idx)[None, :])
    qdec = jnp.exp(lg[:, None] * (idx + 1.0)[None, :])
    kdec = jnp.broadcast_to(kdec[:, :, None], (RET_HEADS, CHUNK, RET_DH))
    qdec = jnp.broadcast_to(qdec[:, :, None], (RET_HEADS, CHUNK, RET_DH))
    cdec = tuple(math.exp(v * CHUNK) for v in log_g)
    t = jnp.arange(CHUNK)[:, None]
    c = jnp.arange(2 * CHUNK)[None, :]
    lag = CHUNK + t - c
    band = jnp.stack([(lag >= 0) & (lag < w) for w in POOL_WINDOWS]).astype(BF16)
    shift = jnp.stack([lag == k for k in range(1, RG_CONV)]).astype(BF16)
    return dmask, kdec, qdec, band, shift, cdec


def _branches(ha, cos, sin, consts, pool_w, pool_scale, conv_w, conv_b, wa, ba, wx, bx, lam, B, S):
    dmask, kdec, qdec, band, shift, cdec = consts
    ts = min(BR_TS, S)
    nst = S // ts
    HA = ha.shape[1]
    OW = RET_W + POOL_W + RG_W
    full = lambda shape: pl.BlockSpec(shape, lambda b, s: (0,) * len(shape))
    return pl.pallas_call(
        functools.partial(_branch_kernel, ts=ts, cdec=cdec),
        out_shape=jax.ShapeDtypeStruct((B * S, OW), BF16),
        grid=(B, nst),
        in_specs=[
            pl.BlockSpec((ts, HA), lambda b, s: (b * nst + s, 0)),
            pl.BlockSpec((ts, RET_DH), lambda b, s: (s, 0)),
            pl.BlockSpec((ts, RET_DH), lambda b, s: (s, 0)),
            full((RET_HEADS, CHUNK, CHUNK)), full((RET_HEADS, CHUNK, RET_DH)), full((RET_HEADS, CHUNK, RET_DH)),
            full(band.shape), full(shift.shape),
            full((len(POOL_WINDOWS), POOL_GW, POOL_GW)), full((1, POOL_W)),
            full((RG_CONV, RG_W)), full((1, RG_W)),
            full((RG_BLOCKS, RG_BD, RG_BD)), full((1, RG_W)),
            full((RG_BLOCKS, RG_BD, RG_BD)), full((1, RG_W)), full((1, RG_W)),
        ],
        out_specs=pl.BlockSpec((ts, OW), lambda b, s: (b * nst + s, 0)),
        scratch_shapes=[
            pltpu.VMEM((RET_HEADS, RET_DH, RET_DH), F32),
            pltpu.VMEM((1, RG_W), F32),
            pltpu.VMEM((2 * CHUNK, POOL_W + RG_W), BF16),
        ],
        compiler_params=_cparams(2),
        name="mixer_branches",
    )(ha, cos, sin, dmask, kdec, qdec, band, shift, pool_w, pool_scale, conv_w, conv_b, wa, ba, wx, bx, lam)


def _merge_kernel(br_ref, gl_ref, x_ref, wbr_ref, wout_ref, gb_ref, lng_ref, lnb_ref, o32_ref, o16_ref, *, alpha):
    D = x_ref.shape[1]
    acc = None
    for i in range(N_BRANCH):
        y = jnp.dot(br_ref[:, i * RET_W:(i + 1) * RET_W], wbr_ref[i], preferred_element_type=F32)
        gate = jax.nn.sigmoid(gl_ref[:, i * D:(i + 1) * D].astype(F32) + gb_ref[i:i + 1, :])
        acc = gate * y if acc is None else acc + gate * y
    m = jnp.dot(acc.astype(BF16), wout_ref[...], preferred_element_type=F32)
    z = _layer_norm(alpha * x_ref[...] + m, lng_ref[...], lnb_ref[...])
    o32_ref[...] = z
    o16_ref[...] = z.astype(BF16)


def _merge(br, gl, x32, w_branch, w_out, gate_b, ln_g, ln_b, alpha):
    T, D = x32.shape
    tm = min(MG_TM, T)
    full = lambda shape: pl.BlockSpec(shape, lambda i: (0,) * len(shape))
    return pl.pallas_call(
        functools.partial(_merge_kernel, alpha=alpha),
        out_shape=(jax.ShapeDtypeStruct((T, D), F32), jax.ShapeDtypeStruct((T, D), BF16)),
        grid=(T // tm,),
        in_specs=[
            pl.BlockSpec((tm, br.shape[1]), lambda i: (i, 0)),
            pl.BlockSpec((tm, gl.shape[1]), lambda i: (i, 0)),
            pl.BlockSpec((tm, D), lambda i: (i, 0)),
            full(w_branch.shape), full(w_out.shape), full(gate_b.shape), full((1, D)), full((1, D)),
        ],
        out_specs=(pl.BlockSpec((tm, D), lambda i: (i, 0)), pl.BlockSpec((tm, D), lambda i: (i, 0))),
        compiler_params=_cparams(1),
        name="mixer_merge",
    )(br, gl, x32, w_branch, w_out, gate_b, ln_g, ln_b)


def _first_argmax(v, iota, n):
    m = jnp.max(v, axis=0, keepdims=True)
    idx = jnp.min(jnp.where(v == m, iota, n), axis=0, keepdims=True)
    return m, idx


def _router_kernel(x_ref, rwt_ref, rb_ref, tri_ref, e_ref, g_ref, rk_ref, cnt_ref):
    E = rwt_ref.shape[0]
    nt = x_ref.shape[0]
    logits = lax.dot_general(rwt_ref[...], x_ref[...], (((1,), (1,)), ((), ())),
                             preferred_element_type=F32, precision=lax.Precision.HIGHEST)
    scores = jax.nn.sigmoid(logits)
    biased = scores + rb_ref[...]

    sub = lax.broadcasted_iota(I32, (E_PER_GROUP, nt), 0)
    giota = lax.broadcasted_iota(I32, (N_GROUPS, nt), 0)
    gs = jnp.zeros((N_GROUPS, nt), F32)
    for g in range(N_GROUPS):
        bg = biased[g * E_PER_GROUP:(g + 1) * E_PER_GROUP, :]
        m1, i1 = _first_argmax(bg, sub, E_PER_GROUP)
        m2 = jnp.max(jnp.where(sub == i1, NEG_INF, bg), axis=0, keepdims=True)
        gs = jnp.where(giota == g, m1 + m2, gs)
    gsel = jnp.zeros((N_GROUPS, nt), jnp.bool_)
    v = gs
    for _ in range(TOPK_GROUPS):
        _, idx = _first_argmax(v, giota, N_GROUPS)
        hit = giota == idx
        gsel = jnp.logical_or(gsel, hit)
        v = jnp.where(hit, NEG_INF, v)
    gself = jnp.where(gsel, 1.0, 0.0)
    masked = jnp.concatenate(
        [jnp.where(jnp.broadcast_to(gself[g:g + 1, :], (E_PER_GROUP, nt)) > 0.5,
                   biased[g * E_PER_GROUP:(g + 1) * E_PER_GROUP, :], NEG_INF) for g in range(N_GROUPS)], axis=0)

    eiota = lax.broadcasted_iota(I32, (E, nt), 0)
    v = masked
    onehot = jnp.zeros((E, nt), F32)
    idxs, sels = [], []
    for _ in range(TOP_K):
        _, idx = _first_argmax(v, eiota, E)
        hit = eiota == idx
        sels.append(jnp.sum(jnp.where(hit, scores, 0.0), axis=0, keepdims=True))
        idxs.append(idx)
        onehot = onehot + jnp.where(hit, 1.0, 0.0)
        v = jnp.where(hit, NEG_INF, v)
    ssum = sels[0]
    for k in range(1, TOP_K):
        ssum = ssum + sels[k]
    excl = jnp.dot(onehot.astype(BF16), tri_ref[...], preferred_element_type=F32)
    for k in range(TOP_K):
        e_ref[k:k + 1, :] = idxs[k]
        g_ref[k:k + 1, :] = sels[k] / ssum * ROUTED_SCALE
        rk_ref[k:k + 1, :] = jnp.sum(jnp.where(eiota == idxs[k], excl, 0.0), axis=0, keepdims=True).astype(I32)
    cnt_ref[...] = jnp.sum(onehot, axis=1, keepdims=True).astype(I32)


def _router(x32, rwt, rb, tri):
    T, D = x32.shape
    E = rwt.shape[0]
    J = T // SUB
    row = lambda dt: jax.ShapeDtypeStruct((TOP_K, T), dt)
    return pl.pallas_call(
        _router_kernel,
        out_shape=(row(I32), row(F32), row(I32), jax.ShapeDtypeStruct((J, E, 1), I32)),
        grid=(J,),
        in_specs=[
            pl.BlockSpec((SUB, D), lambda j: (j, 0)),
            pl.BlockSpec((E, D), lambda j: (0, 0)),
            pl.BlockSpec((E, 1), lambda j: (0, 0)),
            pl.BlockSpec((SUB, SUB), lambda j: (0, 0)),
        ],
        out_specs=(
            pl.BlockSpec((TOP_K, SUB), lambda j: (0, j)),
            pl.BlockSpec((TOP_K, SUB), lambda j: (0, j)),
            pl.BlockSpec((TOP_K, SUB), lambda j: (0, j)),
            pl.BlockSpec((None, E, 1), lambda j: (j, 0, 0)),
        ),
        compiler_params=_cparams(1),
        name="moe_router",
    )(x32, rwt, rb, tri)


def _round_up(a, m):
    return (a + m - 1) // m * m


def _local_rows_max(E):
    return _round_up(SUB * TOP_K + E * (ROW_ALIGN - 1) + LOCAL_ALIGN - ROW_ALIGN, PAIR)


def _num_blocks_max(T, E):
    J = T // SUB
    rows = T * TOP_K + J * (E * (ROW_ALIGN - 1) + LOCAL_ALIGN - ROW_ALIGN) + E * (FFN_BLK - ROW_ALIGN)
    return -(-rows // FFN_BLK)


def _tables(cnt, nblk_max, lmax):
    J, E = cnt.shape
    pc = _round_up(cnt, ROW_ALIGN)
    tot0 = jnp.sum(pc, axis=1)
    pc = pc.at[:, E - 1].add(_round_up(tot0, LOCAL_ALIGN) - tot0)
    lo = jnp.cumsum(pc, axis=1) - pc
    tot = jnp.sum(pc, axis=1)
    se = jnp.sum(pc, axis=0)
    reg = _round_up(se, FFN_BLK)
    creg = jnp.cumsum(reg)
    gs = creg - reg
    go = gs[None, :] + jnp.cumsum(pc, axis=0) - pc
    nused = (creg[-1] // FFN_BLK).reshape(1)
    bstart = jnp.arange(nblk_max, dtype=I32) * FFN_BLK
    bexp = jnp.minimum(jnp.sum(bstart[:, None] >= creg[None, :], axis=1), E - 1)
    trow = jnp.arange(lmax // ROW_ALIGN, dtype=I32) * ROW_ALIGN
    inrun = (trow[None, :, None] >= lo[:, None, :]) & (trow[None, :, None] < (lo + pc)[:, None, :])
    dst = trow[None, :] + jnp.sum(jnp.where(inrun, (go - lo)[:, None, :], 0), axis=2)
    i32 = lambda a: a.astype(I32)
    return dict(lo=i32(lo), dst=i32(dst), tot=i32(tot), tail_start=i32(gs + se), tail_len=i32(reg - se),
                nused=i32(nused), bexp=i32(bexp))


def _wait_rows(src_ref, dst_ref, sem, nrows):
    def wait_unit(unit):
        def body(i, c):
            pltpu.make_async_copy(src_ref.at[pl.ds(0, unit)], dst_ref.at[pl.ds(0, unit)], sem).wait()
            return c
        return body

    lax.fori_loop(0, nrows // SUB, wait_unit(SUB), 0)
    lax.fori_loop(0, (nrows % SUB) // LOCAL_ALIGN, wait_unit(LOCAL_ALIGN), 0)


def _start_tile_copies(copy_tile, dst_s, table_base, tot):
    def body(q, c):
        for u in range(ISSUE_UNROLL):
            i = q * ISSUE_UNROLL + u
            copy_tile(pl.multiple_of(i * ROW_ALIGN, ROW_ALIGN), pl.multiple_of(dst_s[table_base + i], ROW_ALIGN))
        return c

    lax.fori_loop(0, tot // LOCAL_ALIGN, body, 0)


def _dispatch_kernel(dst_s, tot_s, tls_s, tll_s, nu_s,
                     e_ref, rk_ref, lov_ref, x_ref, xs_ref, slot_ref, xbuf, zbuf, sem, *, E, J, LT, nblk_max):
    j = pl.program_id(0)
    par = j % 2
    nt = x_ref.shape[0]

    eiota = lax.broadcasted_iota(I32, (E, nt), 0)
    for k in range(TOP_K):
        base = jnp.sum(jnp.where(eiota == e_ref[k:k + 1, :], lov_ref[...], 0), axis=0, keepdims=True)
        slot_ref[k:k + 1, :] = base + rk_ref[k:k + 1, :]
    slots = slot_ref[...]
    xb = x_ref[...]
    tot = tot_s[j]
    buf = xbuf.at[par]

    def pair(c, carry):
        for u in range(PAIR // SUB):
            base = pl.multiple_of(c * PAIR + u * SUB, SUB)
            siota = lax.broadcasted_iota(I32, (SUB, nt), 0) + base
            p = siota == slots[0:1, :]
            for k in range(1, TOP_K):
                p = jnp.logical_or(p, siota == slots[k:k + 1, :])
            pb = jnp.where(p, 1.0, 0.0).astype(BF16)
            buf[pl.ds(base, SUB), :] = jnp.dot(pb, xb, preferred_element_type=F32).astype(BF16)
        return carry

    lax.fori_loop(0, (tot + PAIR - 1) // PAIR, pair, 0)

    def copy_tile(local_row, sorted_row):
        pltpu.make_async_copy(buf.at[pl.ds(local_row, ROW_ALIGN)], xs_ref.at[pl.ds(sorted_row, ROW_ALIGN)],
                              sem.at[par]).start()

    _start_tile_copies(copy_tile, dst_s, j * LT, tot)

    @pl.when(j == 0)
    def _():
        zbuf[...] = jnp.zeros_like(zbuf)
        ztile = zbuf.at[pl.ds(0, ROW_ALIGN)]

        def per_expert_tail(e, ntile):
            n = tll_s[e] // ROW_ALIGN

            def per_tile(i, c):
                do = pl.multiple_of(tls_s[e] + i * ROW_ALIGN, ROW_ALIGN)
                pltpu.make_async_copy(ztile, xs_ref.at[pl.ds(do, ROW_ALIGN)], sem.at[2]).start()
                return c

            lax.fori_loop(0, n, per_tile, 0)
            return ntile + n

        ntile = lax.fori_loop(0, E, per_expert_tail, 0)

        def per_block(b, c):
            do = pl.multiple_of(b * FFN_BLK, FFN_BLK)
            pltpu.make_async_copy(zbuf, xs_ref.at[pl.ds(do, FFN_BLK)], sem.at[2]).start()
            return c

        lax.fori_loop(nu_s[0], nblk_max, per_block, 0)

        def wtail(i, c):
            pltpu.make_async_copy(ztile, xs_ref.at[pl.ds(0, ROW_ALIGN)], sem.at[2]).wait()
            return c

        def wblock(b, c):
            pltpu.make_async_copy(zbuf, xs_ref.at[pl.ds(0, FFN_BLK)], sem.at[2]).wait()
            return c

        lax.fori_loop(0, ntile, wtail, 0)
        lax.fori_loop(nu_s[0], nblk_max, wblock, 0)

    @pl.when(j > 0)
    def _():
        _wait_rows(xbuf.at[1 - par], xs_ref, sem.at[1 - par], tot_s[j - 1])

    @pl.when(j == J - 1)
    def _():
        _wait_rows(buf, xs_ref, sem.at[par], tot)


def _dispatch(tb, top_e, rank, x16, nblk_max, lmax):
    T, D = x16.shape
    J, E = tb["lo"].shape
    gs = pltpu.PrefetchScalarGridSpec(
        num_scalar_prefetch=5,
        grid=(J,),
        in_specs=[
            pl.BlockSpec((TOP_K, SUB), lambda j, *_: (0, j)),
            pl.BlockSpec((TOP_K, SUB), lambda j, *_: (0, j)),
            pl.BlockSpec((None, E, 1), lambda j, *_: (j, 0, 0)),
            pl.BlockSpec((SUB, D), lambda j, *_: (j, 0)),
        ],
        out_specs=(pl.BlockSpec(memory_space=pl.ANY), pl.BlockSpec((TOP_K, SUB), lambda j, *_: (0, j))),
        scratch_shapes=[
            pltpu.VMEM((2, lmax, D), BF16),
            pltpu.VMEM((FFN_BLK, D), BF16),
            pltpu.SemaphoreType.DMA((3,)),
        ],
    )
    return pl.pallas_call(
        functools.partial(_dispatch_kernel, E=E, J=J, LT=lmax // ROW_ALIGN, nblk_max=nblk_max),
        out_shape=(jax.ShapeDtypeStruct((nblk_max * FFN_BLK, D), BF16), jax.ShapeDtypeStruct((TOP_K, T), I32)),
        grid_spec=gs,
        compiler_params=_cparams(1),
        name="moe_dispatch",
    )(tb["dst"].reshape(-1), tb["tot"], tb["tail_start"], tb["tail_len"], tb["nused"],
      top_e, rank, tb["lo"].reshape(J, E, 1), x16)


def _ffn_kernel(be_s, nu_s, x_ref, w13_ref, w2_ref, y_ref):
    @pl.when(pl.program_id(0) < nu_s[0])
    def _():
        h = jnp.dot(x_ref[...], w13_ref[...], preferred_element_type=F32)
        hh = (jax.nn.silu(h[:, :D_EXPERT]) * h[:, D_EXPERT:]).astype(BF16)
        y_ref[...] = jnp.dot(hh, w2_ref[...], preferred_element_type=F32).astype(BF16)


def _expert_ffn(tb, xs, w13, w2, nblk_max):
    D = xs.shape[1]
    blk = lambda b, be, nu: jnp.minimum(b, nu[0] - 1)
    gs = pltpu.PrefetchScalarGridSpec(
        num_scalar_prefetch=2,
        grid=(nblk_max,),
        in_specs=[
            pl.BlockSpec((FFN_BLK, D), lambda b, be, nu: (blk(b, be, nu), 0)),
            pl.BlockSpec((None, D, 2 * D_EXPERT), lambda b, be, nu: (be[blk(b, be, nu)], 0, 0)),
            pl.BlockSpec((None, D_EXPERT, D), lambda b, be, nu: (be[blk(b, be, nu)], 0, 0)),
        ],
        out_specs=pl.BlockSpec((FFN_BLK, D), lambda b, be, nu: (blk(b, be, nu), 0)),
    )
    return pl.pallas_call(
        _ffn_kernel,
        out_shape=jax.ShapeDtypeStruct(xs.shape, BF16),
        grid_spec=gs,
        input_output_aliases={2: 0},
        compiler_params=_cparams(1),
        name="moe_expert_ffn",
    )(tb["bexp"], tb["nused"], xs, w13, w2)


def _combine_kernel(dst_s, tot_s,
                    slot_ref, g_ref, x32_ref, x16_ref, ys_ref, sw13_ref, sw2_ref, lng_ref, lnb_ref,
                    o32_ref, o16_ref, ybuf, acc_ref, sem, *, LT, alpha):
    j = pl.program_id(0)
    nt = x32_ref.shape[0]
    tot = tot_s[j]

    def copy_tile(local_row, sorted_row):
        pltpu.make_async_copy(ys_ref.at[pl.ds(sorted_row, ROW_ALIGN)], ybuf.at[pl.ds(local_row, ROW_ALIGN)],
                              sem.at[0]).start()

    _start_tile_copies(copy_tile, dst_s, j * LT, tot)

    npair = (tot + PAIR - 1) // PAIR

    def zrows(i, c):
        zo = pl.multiple_of(tot + i * LOCAL_ALIGN, LOCAL_ALIGN)
        ybuf[pl.ds(zo, LOCAL_ALIGN), :] = jnp.zeros((LOCAL_ALIGN, ybuf.shape[1]), BF16)
        return c

    lax.fori_loop(0, (npair * PAIR - tot) // LOCAL_ALIGN, zrows, 0)

    xb = x16_ref[...]
    h = jnp.dot(xb, sw13_ref[...], preferred_element_type=F32)
    hh = (jax.nn.silu(h[:, :D_EXPERT]) * h[:, D_EXPERT:]).astype(BF16)
    acc_ref[...] = jnp.dot(hh, sw2_ref[...], preferred_element_type=F32)

    _wait_rows(ys_ref, ybuf, sem.at[0], tot)

    slots = slot_ref[...]
    gates = g_ref[...]

    def pair(c, carry):
        part = None
        for u in range(PAIR // SUB):
            base = pl.multiple_of(c * PAIR + u * SUB, SUB)
            siota = lax.broadcasted_iota(I32, (SUB, nt), 0) + base
            wt = jnp.where(siota == slots[0:1, :], gates[0:1, :], 0.0)
            for k in range(1, TOP_K):
                wt = wt + jnp.where(siota == slots[k:k + 1, :], gates[k:k + 1, :], 0.0)
            y = lax.dot_general(wt.astype(BF16), ybuf[pl.ds(base, SUB), :], (((0,), (0,)), ((), ())),
                                preferred_element_type=F32)
            part = y if part is None else part + y
        acc_ref[...] += part
        return carry

    lax.fori_loop(0, npair, pair, 0)

    z = _layer_norm(alpha * x32_ref[...] + acc_ref[...], lng_ref[...], lnb_ref[...])
    o32_ref[...] = z
    o16_ref[...] = z.astype(BF16)


def _combine(tb, slot, gate, x32, x16, ys, sw13, sw2, ln_g, ln_b, alpha, lmax):
    T, D = x32.shape
    J = T // SUB
    full = lambda shape: pl.BlockSpec(shape, lambda j, *_: (0,) * len(shape))
    gs = pltpu.PrefetchScalarGridSpec(
        num_scalar_prefetch=2,
        grid=(J,),
        in_specs=[
            pl.BlockSpec((TOP_K, SUB), lambda j, *_: (0, j)),
            pl.BlockSpec((TOP_K, SUB), lambda j, *_: (0, j)),
            pl.BlockSpec((SUB, D), lambda j, *_: (j, 0)),
            pl.BlockSpec((SUB, D), lambda j, *_: (j, 0)),
            pl.BlockSpec(memory_space=pl.ANY),
            full(sw13.shape), full(sw2.shape), full((1, D)), full((1, D)),
        ],
        out_specs=(pl.BlockSpec((SUB, D), lambda j, *_: (j, 0)), pl.BlockSpec((SUB, D), lambda j, *_: (j, 0))),
        scratch_shapes=[
            pltpu.VMEM((lmax, D), BF16),
            pltpu.VMEM((SUB, D), F32),
            pltpu.SemaphoreType.DMA((1,)),
        ],
    )
    return pl.pallas_call(
        functools.partial(_combine_kernel, LT=lmax // ROW_ALIGN, alpha=alpha),
        out_shape=(jax.ShapeDtypeStruct((T, D), F32), jax.ShapeDtypeStruct((T, D), BF16)),
        grid_spec=gs,
        compiler_params=_cparams(1),
        name="moe_combine",
    )(tb["dst"].reshape(-1), tb["tot"], slot, gate, x32, x16, ys, sw13, sw2, ln_g, ln_b)


def kernel(x, positions, w_in, gate_b, w_branch, w_out, pool_w, pool_scale, rg_conv_w, rg_conv_b, rg_wa, rg_ba,
           rg_wx, rg_bx, rg_lambda, ln1_g, ln1_b, router_w, router_bias, exp_w1, exp_w3, exp_w2, sh_w1, sh_w3,
           sh_w2, ln2_g, ln2_b):
    B, S, D = x.shape
    L = w_in.shape[0]
    T = B * S
    E = router_w.shape[2]
    assert T % SUB == 0 and S % CHUNK == 0 and E == N_GROUPS * E_PER_GROUP
    alpha = (2 * L) ** 0.25
    n_a = 4 * RET_W + POOL_W + 2 * RG_W
    nblk_max = _num_blocks_max(T, E)
    lmax = _local_rows_max(E)

    cos, sin = _rope_tables(positions)
    consts = _retention_consts()
    tri = (jnp.arange(SUB)[:, None] < jnp.arange(SUB)[None, :]).astype(BF16)
    row = lambda a: a.reshape(1, -1)

    x32 = x.reshape(T, D)
    x16 = x32.astype(BF16)
    for l in range(L):
        w_in16 = w_in[l].astype(BF16)
        ha = _matmul(x16, w_in16[:, :n_a], "in_proj_a")
        gl = _matmul(x16, w_in16[:, n_a:], "in_proj_gates")
        br = _branches(ha, cos, sin, consts, pool_w[l].astype(BF16), row(pool_scale[l]), rg_conv_w[l],
                       row(rg_conv_b[l]), rg_wa[l].astype(BF16), row(rg_ba[l]), rg_wx[l].astype(BF16),
                       row(rg_bx[l]), row(rg_lambda[l]), B, S)
        x32, x16 = _merge(br, gl, x32, w_branch[l].astype(BF16), w_out[l].astype(BF16), gate_b[l],
                          row(ln1_g[l]), row(ln1_b[l]), alpha)

        top_e, gate, rank, cnt = _router(x32, router_w[l].T, router_bias[l].reshape(E, 1), tri)
        tb = _tables(cnt[:, :, 0], nblk_max, lmax)
        xs, slot = _dispatch(tb, top_e, rank, x16, nblk_max, lmax)
        w13 = jnp.concatenate([exp_w1[l], exp_w3[l]], axis=-1).astype(BF16)
        ys = _expert_ffn(tb, xs, w13, exp_w2[l].astype(BF16), nblk_max)
        sw13 = jnp.concatenate([sh_w1[l], sh_w3[l]], axis=-1).astype(BF16)
        x32, x16 = _combine(tb, slot, gate, x32, x16, ys, sw13, sh_w2[l].astype(BF16),
                            row(ln2_g[l]), row(ln2_b[l]), alpha, lmax)
    return x32.reshape(B, S, D)
```

```python
import functools
import math

import jax
import jax.numpy as jnp
from jax import lax
from jax.experimental import pallas as pl
from jax.experimental.pallas import tpu as pltpu

F32, BF16, I32 = jnp.float32, jnp.bfloat16, jnp.int32

RET_HEADS, RET_DH, RET_W, CHUNK = 4, 128, 512, 128
ROPE_BASE = 10000.0
POOL_WINDOWS, POOL_GW, POOL_W = (2, 4, 8, 16), 128, 512
RG_BLOCKS, RG_BD, RG_W, RG_CONV, RG_C = 4, 128, 512, 4, 8.0
HIST = 16
N_BRANCH = 3
N_GROUPS, E_PER_GROUP, TOPK_GROUPS, TOP_K, D_EXPERT = 8, 8, 4, 8, 256
ROUTED_SCALE = 2.5
LN_EPS, GN_EPS = 1e-5, 1e-6

VMEM_LIMIT = 56 * 1024 * 1024
MM_TM, MM_TN = 2048, 512
BR_TS = 512
MG_TM = 512
SUB = 256
ROW_ALIGN = 16
ISSUE_UNROLL = 4
LOCAL_ALIGN = ROW_ALIGN * ISSUE_UNROLL
PAIR = 2 * SUB
FFN_BLK = 1024
NEG_INF = float("-inf")


def _cparams(n_axes):
    return pltpu.CompilerParams(dimension_semantics=("arbitrary",) * n_axes, vmem_limit_bytes=VMEM_LIMIT)


def _layer_norm(z, g, b):
    mu = jnp.mean(z, axis=-1, keepdims=True)
    zc = z - mu
    var = jnp.mean(zc * zc, axis=-1, keepdims=True)
    return zc * lax.rsqrt(var + LN_EPS) * g + b


def _rope_kernel(pos_ref, inv_ref, sign_ref, cos_ref, sin_ref):
    ang = pos_ref[...].astype(F32) * inv_ref[...]
    cos_ref[...] = jnp.cos(ang)
    sin_ref[...] = jnp.sin(ang) * sign_ref[...]


def _rope_tables(positions):
    S = positions.shape[0]
    half = RET_DH // 2
    inv = ROPE_BASE ** (-jnp.arange(half, dtype=F32) / half)
    inv2 = jnp.concatenate([inv, inv]).reshape(1, RET_DH)
    sign = jnp.concatenate([-jnp.ones((half,), F32), jnp.ones((half,), F32)]).reshape(1, RET_DH)
    return pl.pallas_call(
        _rope_kernel,
        out_shape=(jax.ShapeDtypeStruct((S, RET_DH), F32), jax.ShapeDtypeStruct((S, RET_DH), F32)),
        name="rope_tables",
    )(positions.reshape(S, 1), inv2, sign)


def _mm_kernel(x_ref, w_ref, o_ref):
    o_ref[...] = jnp.dot(x_ref[...], w_ref[...].astype(BF16), preferred_element_type=F32).astype(o_ref.dtype)


def _in_proj(x, w_all, layer, col0, ncols, name):
    M, K = x.shape
    tm, tn = min(MM_TM, M), MM_TN
    assert col0 % tn == 0 and ncols % tn == 0
    return pl.pallas_call(
        _mm_kernel,
        out_shape=jax.ShapeDtypeStruct((M, ncols), BF16),
        grid=(M // tm, ncols // tn),
        in_specs=[pl.BlockSpec((tm, K), lambda i, j: (i, 0)),
                  pl.BlockSpec((None, K, tn), lambda i, j: (layer, 0, j + col0 // tn))],
        out_specs=pl.BlockSpec((tm, tn), lambda i, j: (i, j)),
        compiler_params=_cparams(2),
        name=name,
    )(x, w_all)


def _log_sigmoid(x):
    return jnp.minimum(x, 0.0) - jnp.log(1.0 + jnp.exp(-jnp.abs(x)))


def _branch_kernel(ha_ref, cos_ref, sin_ref, dmask_ref, kdec_ref, qdec_ref, band_ref, shift_ref, poolw_ref,
                   pscale_ref, convw_ref, convb_ref, wa_ref, ba_ref, wx_ref, bx_ref, lam_ref,
                   o_ref, state_ref, hcar_ref, hist_ref, *, ts, cdec):
    s = pl.program_id(1)
    x0 = 4 * RET_W

    @pl.when(s == 0)
    def _():
        state_ref[...] = jnp.zeros_like(state_ref)
        hcar_ref[...] = jnp.zeros_like(hcar_ref)
        hist_ref[...] = jnp.zeros_like(hist_ref)

    log_sig_lam = _log_sigmoid(lam_ref[...])

    def chunk(c, carry):
        r0 = pl.multiple_of(c * CHUNK, CHUNK)
        rows = pl.ds(r0, CHUNK)
        cosf = cos_ref[rows, :]
        sinf = sin_ref[rows, :]

        hist_ref[CHUNK:2 * CHUNK, :] = ha_ref[rows, slice(x0, x0 + POOL_W + RG_W)]
        wins = [jnp.dot(band_ref[gi], hist_ref[:, gi * POOL_GW:(gi + 1) * POOL_GW], preferred_element_type=F32)
                for gi in range(len(POOL_WINDOWS))]
        lagged = [jnp.dot(shift_ref[kk - 1], hist_ref[:, POOL_W:POOL_W + RG_W], preferred_element_type=F32)
                  for kk in range(1, RG_CONV)]
        hist_ref[CHUNK - HIST:CHUNK, :] = hist_ref[2 * CHUNK - HIST:2 * CHUNK, :]

        def retention_head(h):
            cs = slice(h * RET_DH, (h + 1) * RET_DH)
            q = ha_ref[rows, cs].astype(F32)
            k = ha_ref[rows, slice(RET_W + h * RET_DH, RET_W + (h + 1) * RET_DH)].astype(F32)
            v = ha_ref[rows, slice(2 * RET_W + h * RET_DH, 2 * RET_W + (h + 1) * RET_DH)]
            g = ha_ref[rows, slice(3 * RET_W + h * RET_DH, 3 * RET_W + (h + 1) * RET_DH)].astype(F32)
            qr = q * cosf + pltpu.roll(q, RET_DH // 2, 1) * sinf
            kr = (k * cosf + pltpu.roll(k, RET_DH // 2, 1) * sinf) * (RET_DH ** -0.5)
            qb = qr.astype(BF16)
            sc = lax.dot_general(qb, kr.astype(BF16), (((1,), (1,)), ((), ())),
                                 preferred_element_type=F32) * dmask_ref[h]
            o = jnp.dot(sc.astype(BF16), v, preferred_element_type=F32)
            st = state_ref[h]
            o = o + jnp.dot(qb, st.astype(BF16), preferred_element_type=F32) * qdec_ref[h]
            kd = (kr * kdec_ref[h]).astype(BF16)
            state_ref[h] = st * cdec[h] + lax.dot_general(kd, v, (((0,), (0,)), ((), ())),
                                                          preferred_element_type=F32)
            mu = jnp.mean(o, axis=-1, keepdims=True)
            oc = o - mu
            var = jnp.mean(oc * oc, axis=-1, keepdims=True)
            o_ref[rows, cs] = (jax.nn.silu(g) * (oc * lax.rsqrt(var + GN_EPS))).astype(BF16)

        retention_head(0)

        xr = ha_ref[rows, slice(x0 + POOL_W, x0 + POOL_W + RG_W)].astype(F32)
        conv = convb_ref[...] + xr * convw_ref[0:1, :]
        for kk in range(1, RG_CONV):
            conv = conv + lagged[kk - 1] * convw_ref[kk:kk + 1, :]
        cb = conv.astype(BF16)
        rl = jnp.concatenate([jnp.dot(cb[:, n * RG_BD:(n + 1) * RG_BD], wa_ref[n], preferred_element_type=F32)
                              for n in range(RG_BLOCKS)], axis=1)
        il = jnp.concatenate([jnp.dot(cb[:, n * RG_BD:(n + 1) * RG_BD], wx_ref[n], preferred_element_type=F32)
                              for n in range(RG_BLOCKS)], axis=1)

        retention_head(1)

        r = jax.nn.sigmoid(rl + ba_ref[...])
        ig = jax.nn.sigmoid(il + bx_ref[...])
        log_a = RG_C * r * log_sig_lam
        a = jnp.exp(log_a)
        om = -jnp.tanh(log_a) * (a * a + 1.0)
        bb = jnp.where(om > 0.0, om * lax.rsqrt(om), 0.0) * (ig * conv)

        retention_head(2)

        row = lax.broadcasted_iota(I32, (CHUNK, RG_W), 0)
        A, Bv = a, bb
        d = 1
        while d < CHUNK:
            As = jnp.where(row < d, 1.0, pltpu.roll(A, d, 0))
            Bs = jnp.where(row < d, 0.0, pltpu.roll(Bv, d, 0))
            Bv = A * Bs + Bv
            A = A * As
            d *= 2

        retention_head(3)

        hh = A * hcar_ref[...] + Bv
        hcar_ref[...] = hh[CHUNK - 1:CHUNK, :]
        yr = ha_ref[rows, slice(x0 + POOL_W + RG_W, x0 + POOL_W + 2 * RG_W)].astype(F32)
        o_ref[rows, slice(RET_W + POOL_W, RET_W + POOL_W + RG_W)] = (hh * jax.nn.gelu(yr)).astype(BF16)

        xp = ha_ref[rows, slice(x0, x0 + POOL_W)].astype(F32)
        t_idx = s * ts + r0 + lax.broadcasted_iota(I32, (CHUNK, POOL_GW), 0)
        for gi, w in enumerate(POOL_WINDOWS):
            cs = slice(gi * POOL_GW, (gi + 1) * POOL_GW)
            n = jnp.minimum(w, t_idx + 1).astype(F32)
            pooled = wins[gi] / n - xp[:, cs]
            y = jnp.dot(pooled.astype(BF16), poolw_ref[gi], preferred_element_type=F32) * pscale_ref[:, cs]
            o_ref[rows, slice(RET_W + gi * POOL_GW, RET_W + (gi + 1) * POOL_GW)] = y.astype(BF16)
        return carry

    lax.fori_loop(0, ts // CHUNK, chunk, 0, unroll=2)


def _retention_consts():
    log_g = [math.log(1.0 - 2.0 ** (-5.0 - h)) for h in range(RET_HEADS)]
    lg = jnp.asarray(log_g, F32)
    idx = jnp.arange(CHUNK, dtype=F32)
    diff = idx[:, None] - idx[None, :]
    dmask = jnp.where(diff >= 0, jnp.exp(lg[:, None, None] * jnp.maximum(diff, 0.0)), 0.0)
    kdec = jnp.exp(lg[:, None] * (CHUNK - 1 - idx)[None, :])
    qdec = jnp.exp(lg[:, None] * (idx + 1.0)[None, :])
    kdec = jnp.broadcast_to(kdec[:, :, None], (RET_HEADS, CHUNK, RET_DH))
    qdec = jnp.broadcast_to(qdec[:, :, None], (RET_HEADS, CHUNK, RET_DH))
    cdec = tuple(math.exp(v * CHUNK) for v in log_g)
    t = jnp.arange(CHUNK)[:, None]
    c = jnp.arange(2 * CHUNK)[None, :]
    lag = CHUNK + t - c
    band = jnp.stack([(lag >= 0) & (lag < w) for w in POOL_WINDOWS]).astype(BF16)
    shift = jnp.stack([lag == k for k in range(1, RG_CONV)]).astype(BF16)
    return dmask, kdec, qdec, band, shift, cdec


def _branches(ha, cos, sin, consts, pool_w, pool_scale, conv_w, conv_b, wa, ba, wx, bx, lam, B, S):
    dmask, kdec, qdec, band, shift, cdec = consts
    ts = min(BR_TS, S)
    nst = S // ts
    HA = ha.shape[1]
    OW = RET_W + POOL_W + RG_W
    full = lambda shape: pl.BlockSpec(shape, lambda b, s: (0,) * len(shape))
    return pl.pallas_call(
        functools.partial(_branch_kernel, ts=ts, cdec=cdec),
        out_shape=jax.ShapeDtypeStruct((B * S, OW), BF16),
        grid=(B, nst),
        in_specs=[
            pl.BlockSpec((ts, HA), lambda b, s: (b * nst + s, 0)),
            pl.BlockSpec((ts, RET_DH), lambda b, s: (s, 0)),
            pl.BlockSpec((ts, RET_DH), lambda b, s: (s, 0)),
            full((RET_HEADS, CHUNK, CHUNK)), full((RET_HEADS, CHUNK, RET_DH)), full((RET_HEADS, CHUNK, RET_DH)),
            full(band.shape), full(shift.shape),
            full((len(POOL_WINDOWS), POOL_GW, POOL_GW)), full((1, POOL_W)),
            full((RG_CONV, RG_W)), full((1, RG_W)),
            full((RG_BLOCKS, RG_BD, RG_BD)), full((1, RG_W)),
            full((RG_BLOCKS, RG_BD, RG_BD)), full((1, RG_W)), full((1, RG_W)),
        ],
        out_specs=pl.BlockSpec((ts, OW), lambda b, s: (b * nst + s, 0)),
        scratch_shapes=[
            pltpu.VMEM((RET_HEADS, RET_DH, RET_DH), F32),
            pltpu.VMEM((1, RG_W), F32),
            pltpu.VMEM((2 * CHUNK, POOL_W + RG_W), BF16),
        ],
        compiler_params=_cparams(2),
        name="mixer_branches",
    )(ha, cos, sin, dmask, kdec, qdec, band, shift, pool_w, pool_scale, conv_w, conv_b, wa, ba, wx, bx, lam)


def _merge_kernel(br_ref, gl_ref, x_ref, wbr_ref, wout_ref, gb_ref, lng_ref, lnb_ref, o32_ref, o16_ref, *, alpha):
    D = x_ref.shape[1]
    acc = None
    for i in range(N_BRANCH):
        y = jnp.dot(br_ref[:, i * RET_W:(i + 1) * RET_W], wbr_ref[i], preferred_element_type=F32)
        gate = jax.nn.sigmoid(gl_ref[:, i * D:(i + 1) * D].astype(F32) + gb_ref[i:i + 1, :])
        acc = gate * y if acc is None else acc + gate * y
    m = jnp.dot(acc.astype(BF16), wout_ref[...], preferred_element_type=F32)
    z = _layer_norm(alpha * x_ref[...] + m, lng_ref[...], lnb_ref[...])
    o32_ref[...] = z
    o16_ref[...] = z.astype(BF16)


def _merge(br, gl, x32, w_branch, w_out, gate_b, ln_g, ln_b, alpha):
    T, D = x32.shape
    tm = min(MG_TM, T)
    full = lambda shape: pl.BlockSpec(shape, lambda i: (0,) * len(shape))
    return pl.pallas_call(
        functools.partial(_merge_kernel, alpha=alpha),
        out_shape=(jax.ShapeDtypeStruct((T, D), F32), jax.ShapeDtypeStruct((T, D), BF16)),
        grid=(T // tm,),
        in_specs=[
            pl.BlockSpec((tm, br.shape[1]), lambda i: (i, 0)),
            pl.BlockSpec((tm, gl.shape[1]), lambda i: (i, 0)),
            pl.BlockSpec((tm, D), lambda i: (i, 0)),
            full(w_branch.shape), full(w_out.shape), full(gate_b.shape), full((1, D)), full((1, D)),
        ],
        out_specs=(pl.BlockSpec((tm, D), lambda i: (i, 0)), pl.BlockSpec((tm, D), lambda i: (i, 0))),
        compiler_params=_cparams(1),
        name="mixer_merge",
    )(br, gl, x32, w_branch, w_out, gate_b, ln_g, ln_b)


def _first_argmax(v, iota, n):
    m = jnp.max(v, axis=0, keepdims=True)
    idx = jnp.min(jnp.where(v == m, iota, n), axis=0, keepdims=True)
    return m, idx


def _router_kernel(x_ref, rwt_ref, rb_ref, tri_ref, e_ref, g_ref, rk_ref, cnt_ref):
    E = rwt_ref.shape[0]
    nt = x_ref.shape[0]
    logits = lax.dot_general(rwt_ref[...], x_ref[...], (((1,), (1,)), ((), ())),
                             preferred_element_type=F32, precision=lax.Precision.HIGHEST)
    scores = jax.nn.sigmoid(logits)
    biased = scores + rb_ref[...]

    sub = lax.broadcasted_iota(I32, (E_PER_GROUP, nt), 0)
    giota = lax.broadcasted_iota(I32, (N_GROUPS, nt), 0)
    gs = jnp.zeros((N_GROUPS, nt), F32)
    for g in range(N_GROUPS):
        bg = biased[g * E_PER_GROUP:(g + 1) * E_PER_GROUP, :]
        m1, i1 = _first_argmax(bg, sub, E_PER_GROUP)
        m2 = jnp.max(jnp.where(sub == i1, NEG_INF, bg), axis=0, keepdims=True)
        gs = jnp.where(giota == g, m1 + m2, gs)
    gsel = jnp.zeros((N_GROUPS, nt), jnp.bool_)
    v = gs
    for _ in range(TOPK_GROUPS):
        _, idx = _first_argmax(v, giota, N_GROUPS)
        hit = giota == idx
        gsel = jnp.logical_or(gsel, hit)
        v = jnp.where(hit, NEG_INF, v)
    gself = jnp.where(gsel, 1.0, 0.0)
    masked = jnp.concatenate(
        [jnp.where(jnp.broadcast_to(gself[g:g + 1, :], (E_PER_GROUP, nt)) > 0.5,
                   biased[g * E_PER_GROUP:(g + 1) * E_PER_GROUP, :], NEG_INF) for g in range(N_GROUPS)], axis=0)

    eiota = lax.broadcasted_iota(I32, (E, nt), 0)
    v = masked
    onehot = jnp.zeros((E, nt), F32)
    idxs, sels = [], []
    for _ in range(TOP_K):
        _, idx = _first_argmax(v, eiota, E)
        hit = eiota == idx
        sels.append(jnp.sum(jnp.where(hit, scores, 0.0), axis=0, keepdims=True))
        idxs.append(idx)
        onehot = onehot + jnp.where(hit, 1.0, 0.0)
        v = jnp.where(hit, NEG_INF, v)
    ssum = sels[0]
    for k in range(1, TOP_K):
        ssum = ssum + sels[k]
    excl = jnp.dot(onehot.astype(BF16), tri_ref[...], preferred_element_type=F32)
    for k in range(TOP_K):
        e_ref[k:k + 1, :] = idxs[k]
        g_ref[k:k + 1, :] = sels[k] / ssum * ROUTED_SCALE
        rk_ref[k:k + 1, :] = jnp.sum(jnp.where(eiota == idxs[k], excl, 0.0), axis=0, keepdims=True).astype(I32)
    cnt_ref[...] = jnp.sum(onehot, axis=1, keepdims=True).astype(I32)


def _router(x32, rwt, rb, tri):
    T, D = x32.shape
    E = rwt.shape[0]
    J = T // SUB
    row = lambda dt: jax.ShapeDtypeStruct((TOP_K, T), dt)
    return pl.pallas_call(
        _router_kernel,
        out_shape=(row(I32), row(F32), row(I32), jax.ShapeDtypeStruct((J, E, 1), I32)),
        grid=(J,),
        in_specs=[
            pl.BlockSpec((SUB, D), lambda j: (j, 0)),
            pl.BlockSpec((E, D), lambda j: (0, 0)),
            pl.BlockSpec((E, 1), lambda j: (0, 0)),
            pl.BlockSpec((SUB, SUB), lambda j: (0, 0)),
        ],
        out_specs=(
            pl.BlockSpec((TOP_K, SUB), lambda j: (0, j)),
            pl.BlockSpec((TOP_K, SUB), lambda j: (0, j)),
            pl.BlockSpec((TOP_K, SUB), lambda j: (0, j)),
            pl.BlockSpec((None, E, 1), lambda j: (j, 0, 0)),
        ),
        compiler_params=_cparams(1),
        name="moe_router",
    )(x32, rwt, rb, tri)


def _round_up(a, m):
    return (a + m - 1) // m * m


def _local_rows_max(E):
    return _round_up(SUB * TOP_K + E * (ROW_ALIGN - 1) + LOCAL_ALIGN - ROW_ALIGN, PAIR)


def _num_blocks_max(T, E):
    J = T // SUB
    rows = T * TOP_K + J * (E * (ROW_ALIGN - 1) + LOCAL_ALIGN - ROW_ALIGN) + E * (FFN_BLK - ROW_ALIGN)
    return -(-rows // FFN_BLK)


def _tables(cnt, nblk_max, lmax):
    J, E = cnt.shape
    pc = _round_up(cnt, ROW_ALIGN)
    tot0 = jnp.sum(pc, axis=1)
    pc = pc.at[:, E - 1].add(_round_up(tot0, LOCAL_ALIGN) - tot0)
    lo = jnp.cumsum(pc, axis=1) - pc
    tot = jnp.sum(pc, axis=1)
    se = jnp.sum(pc, axis=0)
    reg = _round_up(se, FFN_BLK)
    creg = jnp.cumsum(reg)
    gs = creg - reg
    go = gs[None, :] + jnp.cumsum(pc, axis=0) - pc
    nused = (creg[-1] // FFN_BLK).reshape(1)
    bstart = jnp.arange(nblk_max, dtype=I32) * FFN_BLK
    bexp = jnp.minimum(jnp.sum(bstart[:, None] >= creg[None, :], axis=1), E - 1)
    trow = jnp.arange(lmax // ROW_ALIGN, dtype=I32) * ROW_ALIGN
    inrun = (trow[None, :, None] >= lo[:, None, :]) & (trow[None, :, None] < (lo + pc)[:, None, :])
    dst = trow[None, :] + jnp.sum(jnp.where(inrun, (go - lo)[:, None, :], 0), axis=2)
    i32 = lambda a: a.astype(I32)
    return dict(lo=i32(lo), dst=i32(dst), tot=i32(tot), tail_start=i32(gs + se), tail_len=i32(reg - se),
                nused=i32(nused), bexp=i32(bexp))


def _wait_rows(src_ref, dst_ref, sem, nrows):
    def wait_unit(unit):
        def body(i, c):
            pltpu.make_async_copy(src_ref.at[pl.ds(0, unit)], dst_ref.at[pl.ds(0, unit)], sem).wait()
            return c
        return body

    lax.fori_loop(0, nrows // SUB, wait_unit(SUB), 0)
    lax.fori_loop(0, (nrows % SUB) // LOCAL_ALIGN, wait_unit(LOCAL_ALIGN), 0)


def _start_tile_copies(copy_tile, dst_s, table_base, tot):
    def body(q, c):
        for u in range(ISSUE_UNROLL):
            i = q * ISSUE_UNROLL + u
            copy_tile(pl.multiple_of(i * ROW_ALIGN, ROW_ALIGN), pl.multiple_of(dst_s[table_base + i], ROW_ALIGN))
        return c

    lax.fori_loop(0, tot // LOCAL_ALIGN, body, 0)


def _dispatch_kernel(dst_s, tot_s, tls_s, tll_s, nu_s,
                     e_ref, rk_ref, lov_ref, x_ref, xs_ref, slot_ref, xbuf, zbuf, sem, *, E, J, LT, nblk_max):
    j = pl.program_id(0)
    par = j % 2
    nt = x_ref.shape[0]

    eiota = lax.broadcasted_iota(I32, (E, nt), 0)
    for k in range(TOP_K):
        base = jnp.sum(jnp.where(eiota == e_ref[k:k + 1, :], lov_ref[...], 0), axis=0, keepdims=True)
        slot_ref[k:k + 1, :] = base + rk_ref[k:k + 1, :]
    slots = slot_ref[...]
    xb = x_ref[...]
    tot = tot_s[j]
    buf = xbuf.at[par]

    def pair(c, carry):
        for u in range(PAIR // SUB):
            base = pl.multiple_of(c * PAIR + u * SUB, SUB)
            siota = lax.broadcasted_iota(I32, (SUB, nt), 0) + base
            p = siota == slots[0:1, :]
            for k in range(1, TOP_K):
                p = jnp.logical_or(p, siota == slots[k:k + 1, :])
            pb = jnp.where(p, 1.0, 0.0).astype(BF16)
            buf[pl.ds(base, SUB), :] = jnp.dot(pb, xb, preferred_element_type=F32).astype(BF16)
        return carry

    lax.fori_loop(0, (tot + PAIR - 1) // PAIR, pair, 0)

    def copy_tile(local_row, sorted_row):
        pltpu.make_async_copy(buf.at[pl.ds(local_row, ROW_ALIGN)], xs_ref.at[pl.ds(sorted_row, ROW_ALIGN)],
                              sem.at[par]).start()

    _start_tile_copies(copy_tile, dst_s, j * LT, tot)

    @pl.when(j == 0)
    def _():
        zbuf[...] = jnp.zeros_like(zbuf)
        ztile = zbuf.at[pl.ds(0, ROW_ALIGN)]

        def per_expert_tail(e, ntile):
            n = tll_s[e] // ROW_ALIGN

            def per_tile(i, c):
                do = pl.multiple_of(tls_s[e] + i * ROW_ALIGN, ROW_ALIGN)
                pltpu.make_async_copy(ztile, xs_ref.at[pl.ds(do, ROW_ALIGN)], sem.at[2]).start()
                return c

            lax.fori_loop(0, n, per_tile, 0)
            return ntile + n

        ntile = lax.fori_loop(0, E, per_expert_tail, 0)

        def per_block(b, c):
            do = pl.multiple_of(b * FFN_BLK, FFN_BLK)
            pltpu.make_async_copy(zbuf, xs_ref.at[pl.ds(do, FFN_BLK)], sem.at[2]).start()
            return c

        lax.fori_loop(nu_s[0], nblk_max, per_block, 0)

        def wtail(i, c):
            pltpu.make_async_copy(ztile, xs_ref.at[pl.ds(0, ROW_ALIGN)], sem.at[2]).wait()
            return c

        def wblock(b, c):
            pltpu.make_async_copy(zbuf, xs_ref.at[pl.ds(0, FFN_BLK)], sem.at[2]).wait()
            return c

        lax.fori_loop(0, ntile, wtail, 0)
        lax.fori_loop(nu_s[0], nblk_max, wblock, 0)

    @pl.when(j > 0)
    def _():
        _wait_rows(xbuf.at[1 - par], xs_ref, sem.at[1 - par], tot_s[j - 1])

    @pl.when(j == J - 1)
    def _():
        _wait_rows(buf, xs_ref, sem.at[par], tot)


def _dispatch(tb, top_e, rank, x16, nblk_max, lmax):
    T, D = x16.shape
    J, E = tb["lo"].shape
    gs = pltpu.PrefetchScalarGridSpec(
        num_scalar_prefetch=5,
        grid=(J,),
        in_specs=[
            pl.BlockSpec((TOP_K, SUB), lambda j, *_: (0, j)),
            pl.BlockSpec((TOP_K, SUB), lambda j, *_: (0, j)),
            pl.BlockSpec((None, E, 1), lambda j, *_: (j, 0, 0)),
            pl.BlockSpec((SUB, D), lambda j, *_: (j, 0)),
        ],
        out_specs=(pl.BlockSpec(memory_space=pl.ANY), pl.BlockSpec((TOP_K, SUB), lambda j, *_: (0, j))),
        scratch_shapes=[
            pltpu.VMEM((2, lmax, D), BF16),
            pltpu.VMEM((FFN_BLK, D), BF16),
            pltpu.SemaphoreType.DMA((3,)),
        ],
    )
    return pl.pallas_call(
        functools.partial(_dispatch_kernel, E=E, J=J, LT=lmax // ROW_ALIGN, nblk_max=nblk_max),
        out_shape=(jax.ShapeDtypeStruct((nblk_max * FFN_BLK, D), BF16), jax.ShapeDtypeStruct((TOP_K, T), I32)),
        grid_spec=gs,
        compiler_params=_cparams(1),
        name="moe_dispatch",
    )(tb["dst"].reshape(-1), tb["tot"], tb["tail_start"], tb["tail_len"], tb["nused"],
      top_e, rank, tb["lo"].reshape(J, E, 1), x16)


def _ffn_kernel(be_s, nu_s, x_ref, w1_ref, w3_ref, w2_ref, y_ref):
    @pl.when(pl.program_id(0) < nu_s[0])
    def _():
        xb = x_ref[...]
        h1 = jnp.dot(xb, w1_ref[...].astype(BF16), preferred_element_type=F32)
        h3 = jnp.dot(xb, w3_ref[...].astype(BF16), preferred_element_type=F32)
        hh = (jax.nn.silu(h1) * h3).astype(BF16)
        y_ref[...] = jnp.dot(hh, w2_ref[...].astype(BF16), preferred_element_type=F32).astype(BF16)


def _expert_ffn(tb, xs, w1, w3, w2, layer, nblk_max):
    D = xs.shape[1]
    blk = lambda b, be, nu: jnp.minimum(b, nu[0] - 1)
    wspec = lambda shape: pl.BlockSpec((None, None) + shape, lambda b, be, nu: (layer, be[blk(b, be, nu)], 0, 0))
    gs = pltpu.PrefetchScalarGridSpec(
        num_scalar_prefetch=2,
        grid=(nblk_max,),
        in_specs=[
            pl.BlockSpec((FFN_BLK, D), lambda b, be, nu: (blk(b, be, nu), 0)),
            wspec((D, D_EXPERT)), wspec((D, D_EXPERT)), wspec((D_EXPERT, D)),
        ],
        out_specs=pl.BlockSpec((FFN_BLK, D), lambda b, be, nu: (blk(b, be, nu), 0)),
    )
    return pl.pallas_call(
        _ffn_kernel,
        out_shape=jax.ShapeDtypeStruct(xs.shape, BF16),
        grid_spec=gs,
        input_output_aliases={2: 0},
        compiler_params=_cparams(1),
        name="moe_expert_ffn",
    )(tb["bexp"], tb["nused"], xs, w1, w3, w2)


def _combine_kernel(dst_s, tot_s,
                    slot_ref, g_ref, x32_ref, x16_ref, ys_ref, sw13_ref, sw2_ref, lng_ref, lnb_ref,
                    o32_ref, o16_ref, ybuf, acc_ref, sem, *, J, LT, alpha):
    j = pl.program_id(0)
    par = j % 2
    nt = x32_ref.shape[0]
    tot = tot_s[j]
    buf = ybuf.at[par]

    def start_gather(step, slot):
        def copy_tile(local_row, sorted_row):
            pltpu.make_async_copy(ys_ref.at[pl.ds(sorted_row, ROW_ALIGN)],
                                  ybuf.at[slot, pl.ds(local_row, ROW_ALIGN)], sem.at[slot]).start()

        _start_tile_copies(copy_tile, dst_s, step * LT, tot_s[step])

    @pl.when(j == 0)
    def _():
        start_gather(0, 0)

    @pl.when(j + 1 < J)
    def _():
        start_gather(j + 1, 1 - par)

    npair = (tot + PAIR - 1) // PAIR

    def zrows(i, c):
        zo = pl.multiple_of(tot + i * LOCAL_ALIGN, LOCAL_ALIGN)
        buf[pl.ds(zo, LOCAL_ALIGN), :] = jnp.zeros((LOCAL_ALIGN, ybuf.shape[2]), BF16)
        return c

    lax.fori_loop(0, (npair * PAIR - tot) // LOCAL_ALIGN, zrows, 0)

    xb = x16_ref[...]
    h = jnp.dot(xb, sw13_ref[...], preferred_element_type=F32)
    hh = (jax.nn.silu(h[:, :D_EXPERT]) * h[:, D_EXPERT:]).astype(BF16)
    acc_ref[...] = jnp.dot(hh, sw2_ref[...], preferred_element_type=F32)

    _wait_rows(ys_ref, buf, sem.at[par], tot)

    slots = slot_ref[...]
    gates = g_ref[...]

    def pair(c, carry):
        part = None
        for u in range(PAIR // SUB):
            base = pl.multiple_of(c * PAIR + u * SUB, SUB)
            siota = lax.broadcasted_iota(I32, (SUB, nt), 0) + base
            wt = jnp.where(siota == slots[0:1, :], gates[0:1, :], 0.0)
            for k in range(1, TOP_K):
                wt = wt + jnp.where(siota == slots[k:k + 1, :], gates[k:k + 1, :], 0.0)
            y = lax.dot_general(wt.astype(BF16), buf[pl.ds(base, SUB), :], (((0,), (0,)), ((), ())),
                                preferred_element_type=F32)
            part = y if part is None else part + y
        acc_ref[...] += part
        return carry

    lax.fori_loop(0, npair, pair, 0)

    z = _layer_norm(alpha * x32_ref[...] + acc_ref[...], lng_ref[...], lnb_ref[...])
    o32_ref[...] = z
    o16_ref[...] = z.astype(BF16)


def _combine(tb, slot, gate, x32, x16, ys, sw13, sw2, ln_g, ln_b, alpha, lmax):
    T, D = x32.shape
    J = T // SUB
    full = lambda shape: pl.BlockSpec(shape, lambda j, *_: (0,) * len(shape))
    gs = pltpu.PrefetchScalarGridSpec(
        num_scalar_prefetch=2,
        grid=(J,),
        in_specs=[
            pl.BlockSpec((TOP_K, SUB), lambda j, *_: (0, j)),
            pl.BlockSpec((TOP_K, SUB), lambda j, *_: (0, j)),
            pl.BlockSpec((SUB, D), lambda j, *_: (j, 0)),
            pl.BlockSpec((SUB, D), lambda j, *_: (j, 0)),
            pl.BlockSpec(memory_space=pl.ANY),
            full(sw13.shape), full(sw2.shape), full((1, D)), full((1, D)),
        ],
        out_specs=(pl.BlockSpec((SUB, D), lambda j, *_: (j, 0)), pl.BlockSpec((SUB, D), lambda j, *_: (j, 0))),
        scratch_shapes=[
            pltpu.VMEM((2, lmax, D), BF16),
            pltpu.VMEM((SUB, D), F32),
            pltpu.SemaphoreType.DMA((2,)),
        ],
    )
    return pl.pallas_call(
        functools.partial(_combine_kernel, J=J, LT=lmax // ROW_ALIGN, alpha=alpha),
        out_shape=(jax.ShapeDtypeStruct((T, D), F32), jax.ShapeDtypeStruct((T, D), BF16)),
        grid_spec=gs,
        compiler_params=_cparams(1),
        name="moe_combine",
    )(tb["dst"].reshape(-1), tb["tot"], slot, gate, x32, x16, ys, sw13, sw2, ln_g, ln_b)


def kernel(x, positions, w_in, gate_b, w_branch, w_out, pool_w, pool_scale, rg_conv_w, rg_conv_b, rg_wa, rg_ba,
           rg_wx, rg_bx, rg_lambda, ln1_g, ln1_b, router_w, router_bias, exp_w1, exp_w3, exp_w2, sh_w1, sh_w3,
           sh_w2, ln2_g, ln2_b):
    B, S, D = x.shape
    L = w_in.shape[0]
    T = B * S
    E = router_w.shape[2]
    assert T % SUB == 0 and S % CHUNK == 0 and E == N_GROUPS * E_PER_GROUP
    alpha = (2 * L) ** 0.25
    n_a = 4 * RET_W + POOL_W + 2 * RG_W
    nblk_max = _num_blocks_max(T, E)
    lmax = _local_rows_max(E)

    cos, sin = _rope_tables(positions)
    consts = _retention_consts()
    tri = (jnp.arange(SUB)[:, None] < jnp.arange(SUB)[None, :]).astype(BF16)
    row = lambda a: a.reshape(1, -1)

    x32 = x.reshape(T, D)
    x16 = x32.astype(BF16)
    for l in range(L):
        ha = _in_proj(x16, w_in, l, 0, n_a, "in_proj_a")
        gl = _in_proj(x16, w_in, l, n_a, w_in.shape[2] - n_a, "in_proj_gates")
        br = _branches(ha, cos, sin, consts, pool_w[l].astype(BF16), row(pool_scale[l]), rg_conv_w[l],
                       row(rg_conv_b[l]), rg_wa[l].astype(BF16), row(rg_ba[l]), rg_wx[l].astype(BF16),
                       row(rg_bx[l]), row(rg_lambda[l]), B, S)
        x32, x16 = _merge(br, gl, x32, w_branch[l].astype(BF16), w_out[l].astype(BF16), gate_b[l],
                          row(ln1_g[l]), row(ln1_b[l]), alpha)

        top_e, gate, rank, cnt = _router(x32, router_w[l].T, router_bias[l].reshape(E, 1), tri)
        tb = _tables(cnt[:, :, 0], nblk_max, lmax)
        xs, slot = _dispatch(tb, top_e, rank, x16, nblk_max, lmax)
        ys = _expert_ffn(tb, xs, exp_w1, exp_w3, exp_w2, l, nblk_max)
        sw13 = jnp.concatenate([sh_w1[l], sh_w3[l]], axis=-1).astype(BF16)
        x32, x16 = _combine(tb, slot, gate, x32, x16, ys, sw13, sh_w2[l].astype(BF16),
                            row(ln2_g[l]), row(ln2_b[l]), alpha, lmax)
    return x32.reshape(B, S, D)
```

```python
import functools
import math

import jax
import jax.numpy as jnp
from jax import lax
from jax.experimental import pallas as pl
from jax.experimental.pallas import tpu as pltpu

F32, BF16, I32 = jnp.float32, jnp.bfloat16, jnp.int32

RET_HEADS, RET_DH, RET_W, CHUNK = 4, 128, 512, 128
ROPE_BASE = 10000.0
POOL_WINDOWS, POOL_GW, POOL_W = (2, 4, 8, 16), 128, 512
RG_BLOCKS, RG_BD, RG_W, RG_CONV, RG_C = 4, 128, 512, 4, 8.0
HIST = 16
N_BRANCH = 3
N_GROUPS, E_PER_GROUP, TOPK_GROUPS, TOP_K, D_EXPERT = 8, 8, 4, 8, 256
ROUTED_SCALE = 2.5
LN_EPS, GN_EPS = 1e-5, 1e-6

VMEM_LIMIT = 56 * 1024 * 1024
MM_TM, MM_TN = 2048, 512
BR_TS = 512
MG_TM = 512
SUB = 256
ROW_ALIGN = 16
ISSUE_UNROLL = 4
LOCAL_ALIGN = ROW_ALIGN * ISSUE_UNROLL
PAIR = 2 * SUB
FFN_BLK = 1024
NEG_INF = float("-inf")


def _cparams(n_axes):
    return pltpu.CompilerParams(dimension_semantics=("arbitrary",) * n_axes, vmem_limit_bytes=VMEM_LIMIT)


def _layer_norm(z, g, b):
    mu = jnp.mean(z, axis=-1, keepdims=True)
    zc = z - mu
    var = jnp.mean(zc * zc, axis=-1, keepdims=True)
    return zc * lax.rsqrt(var + LN_EPS) * g + b


def _rope_kernel(pos_ref, inv_ref, sign_ref, cos_ref, sin_ref):
    ang = pos_ref[...].astype(F32) * inv_ref[...]
    cos_ref[...] = jnp.cos(ang)
    sin_ref[...] = jnp.sin(ang) * sign_ref[...]


def _rope_tables(positions):
    S = positions.shape[0]
    half = RET_DH // 2
    inv = ROPE_BASE ** (-jnp.arange(half, dtype=F32) / half)
    inv2 = jnp.concatenate([inv, inv]).reshape(1, RET_DH)
    sign = jnp.concatenate([-jnp.ones((half,), F32), jnp.ones((half,), F32)]).reshape(1, RET_DH)
    return pl.pallas_call(
        _rope_kernel,
        out_shape=(jax.ShapeDtypeStruct((S, RET_DH), F32), jax.ShapeDtypeStruct((S, RET_DH), F32)),
        name="rope_tables",
    )(positions.reshape(S, 1), inv2, sign)


def _mm_kernel(x_ref, w_ref, o_ref):
    o_ref[...] = jnp.dot(x_ref[...], w_ref[...].astype(BF16), preferred_element_type=F32).astype(o_ref.dtype)


def _in_proj(x, w_all, layer, col0, ncols, name):
    M, K = x.shape
    tm, tn = min(MM_TM, M), MM_TN
    assert col0 % tn == 0 and ncols % tn == 0
    return pl.pallas_call(
        _mm_kernel,
        out_shape=jax.ShapeDtypeStruct((M, ncols), BF16),
        grid=(M // tm, ncols // tn),
        in_specs=[pl.BlockSpec((tm, K), lambda i, j: (i, 0)),
                  pl.BlockSpec((None, K, tn), lambda i, j: (layer, 0, j + col0 // tn))],
        out_specs=pl.BlockSpec((tm, tn), lambda i, j: (i, j)),
        compiler_params=_cparams(2),
        name=name,
    )(x, w_all)


def _log_sigmoid(x):
    return jnp.minimum(x, 0.0) - jnp.log(1.0 + jnp.exp(-jnp.abs(x)))


def _branch_kernel(ha_ref, cos_ref, sin_ref, dmask_ref, kdec_ref, qdec_ref, band_ref, shift_ref, poolw_ref,
                   pscale_ref, convw_ref, convb_ref, wa_ref, ba_ref, wx_ref, bx_ref, lam_ref,
                   o_ref, state_ref, hcar_ref, hist_ref, *, ts, cdec):
    s = pl.program_id(1)
    x0 = 4 * RET_W

    @pl.when(s == 0)
    def _():
        state_ref[...] = jnp.zeros_like(state_ref)
        hcar_ref[...] = jnp.zeros_like(hcar_ref)
        hist_ref[...] = jnp.zeros_like(hist_ref)

    log_sig_lam = _log_sigmoid(lam_ref[...])

    def chunk(c, carry):
        r0 = pl.multiple_of(c * CHUNK, CHUNK)
        rows = pl.ds(r0, CHUNK)
        cosf = cos_ref[rows, :]
        sinf = sin_ref[rows, :]

        hist_ref[CHUNK:2 * CHUNK, :] = ha_ref[rows, slice(x0, x0 + POOL_W + RG_W)]
        wins = [jnp.dot(band_ref[gi], hist_ref[:, gi * POOL_GW:(gi + 1) * POOL_GW], preferred_element_type=F32)
                for gi in range(len(POOL_WINDOWS))]
        lagged = [jnp.dot(shift_ref[kk - 1], hist_ref[:, POOL_W:POOL_W + RG_W], preferred_element_type=F32)
                  for kk in range(1, RG_CONV)]
        hist_ref[CHUNK - HIST:CHUNK, :] = hist_ref[2 * CHUNK - HIST:2 * CHUNK, :]

        def retention_head(h):
            cs = slice(h * RET_DH, (h + 1) * RET_DH)
            q = ha_ref[rows, cs].astype(F32)
            k = ha_ref[rows, slice(RET_W + h * RET_DH, RET_W + (h + 1) * RET_DH)].astype(F32)
            v = ha_ref[rows, slice(2 * RET_W + h * RET_DH, 2 * RET_W + (h + 1) * RET_DH)]
            g = ha_ref[rows, slice(3 * RET_W + h * RET_DH, 3 * RET_W + (h + 1) * RET_DH)].astype(F32)
            qr = q * cosf + pltpu.roll(q, RET_DH // 2, 1) * sinf
            kr = (k * cosf + pltpu.roll(k, RET_DH // 2, 1) * sinf) * (RET_DH ** -0.5)
            qb = qr.astype(BF16)
            sc = lax.dot_general(qb, kr.astype(BF16), (((1,), (1,)), ((), ())),
                                 preferred_element_type=F32) * dmask_ref[h]
            o = jnp.dot(sc.astype(BF16), v, preferred_element_type=F32)
            st = state_ref[h]
            o = o + jnp.dot(qb, st.astype(BF16), preferred_element_type=F32) * qdec_ref[h]
            kd = (kr * kdec_ref[h]).astype(BF16)
            state_ref[h] = st * cdec[h] + lax.dot_general(kd, v, (((0,), (0,)), ((), ())),
                                                          preferred_element_type=F32)
            mu = jnp.mean(o, axis=-1, keepdims=True)
            oc = o - mu
            var = jnp.mean(oc * oc, axis=-1, keepdims=True)
            o_ref[rows, cs] = (jax.nn.silu(g) * (oc * lax.rsqrt(var + GN_EPS))).astype(BF16)

        retention_head(0)

        xr = ha_ref[rows, slice(x0 + POOL_W, x0 + POOL_W + RG_W)].astype(F32)
        conv = convb_ref[...] + xr * convw_ref[0:1, :]
        for kk in range(1, RG_CONV):
            conv = conv + lagged[kk - 1] * convw_ref[kk:kk + 1, :]
        cb = conv.astype(BF16)
        rl = jnp.concatenate([jnp.dot(cb[:, n * RG_BD:(n + 1) * RG_BD], wa_ref[n], preferred_element_type=F32)
                              for n in range(RG_BLOCKS)], axis=1)
        il = jnp.concatenate([jnp.dot(cb[:, n * RG_BD:(n + 1) * RG_BD], wx_ref[n], preferred_element_type=F32)
                              for n in range(RG_BLOCKS)], axis=1)

        retention_head(1)

        r = jax.nn.sigmoid(rl + ba_ref[...])
        ig = jax.nn.sigmoid(il + bx_ref[...])
        log_a = RG_C * r * log_sig_lam
        a = jnp.exp(log_a)
        om = -jnp.tanh(log_a) * (a * a + 1.0)
        bb = jnp.where(om > 0.0, om * lax.rsqrt(om), 0.0) * (ig * conv)

        retention_head(2)

        row = lax.broadcasted_iota(I32, (CHUNK, RG_W), 0)
        A, Bv = a, bb
        d = 1
        while d < CHUNK:
            As = jnp.where(row < d, 1.0, pltpu.roll(A, d, 0))
            Bs = jnp.where(row < d, 0.0, pltpu.roll(Bv, d, 0))
            Bv = A * Bs + Bv
            A = A * As
            d *= 2

        retention_head(3)

        hh = A * hcar_ref[...] + Bv
        hcar_ref[...] = hh[CHUNK - 1:CHUNK, :]
        yr = ha_ref[rows, slice(x0 + POOL_W + RG_W, x0 + POOL_W + 2 * RG_W)].astype(F32)
        o_ref[rows, slice(RET_W + POOL_W, RET_W + POOL_W + RG_W)] = (hh * jax.nn.gelu(yr)).astype(BF16)

        xp = ha_ref[rows, slice(x0, x0 + POOL_W)].astype(F32)
        t_idx = s * ts + r0 + lax.broadcasted_iota(I32, (CHUNK, POOL_GW), 0)
        for gi, w in enumerate(POOL_WINDOWS):
            cs = slice(gi * POOL_GW, (gi + 1) * POOL_GW)
            n = jnp.minimum(w, t_idx + 1).astype(F32)
            pooled = wins[gi] / n - xp[:, cs]
            y = jnp.dot(pooled.astype(BF16), poolw_ref[gi], preferred_element_type=F32) * pscale_ref[:, cs]
            o_ref[rows, slice(RET_W + gi * POOL_GW, RET_W + (gi + 1) * POOL_GW)] = y.astype(BF16)
        return carry

    lax.fori_loop(0, ts // CHUNK, chunk, 0, unroll=2)


def _retention_consts():
    log_g = [math.log(1.0 - 2.0 ** (-5.0 - h)) for h in range(RET_HEADS)]
    lg = jnp.asarray(log_g, F32)
    idx = jnp.arange(CHUNK, dtype=F32)
    diff = idx[:, None] - idx[None, :]
    dmask = jnp.where(diff >= 0, jnp.exp(lg[:, None, None] * jnp.maximum(diff, 0.0)), 0.0)
    kdec = jnp.exp(lg[:, None] * (CHUNK - 1 - idx)[None, :])
    qdec = jnp.exp(lg[:, None] * (idx + 1.0)[None, :])
    kdec = jnp.broadcast_to(kdec[:, :, None], (RET_HEADS, CHUNK, RET_DH))
    qdec = jnp.broadcast_to(qdec[:, :, None], (RET_HEADS, CHUNK, RET_DH))
    cdec = tuple(math.exp(v * CHUNK) for v in log_g)
    t = jnp.arange(CHUNK)[:, None]
    c = jnp.arange(2 * CHUNK)[None, :]
    lag = CHUNK + t - c
    band = jnp.stack([(lag >= 0) & (lag < w) for w in POOL_WINDOWS]).astype(BF16)
    shift = jnp.stack([lag == k for k in range(1, RG_CONV)]).astype(BF16)
    return dmask, kdec, qdec, band, shift, cdec


def _branches(ha, cos, sin, consts, pool_w, pool_scale, conv_w, conv_b, wa, ba, wx, bx, lam, B, S):
    dmask, kdec, qdec, band, shift, cdec = consts
    ts = min(BR_TS, S)
    nst = S // ts
    HA = ha.shape[1]
    OW = RET_W + POOL_W + RG_W
    full = lambda shape: pl.BlockSpec(shape, lambda b, s: (0,) * len(shape))
    return pl.pallas_call(
        functools.partial(_branch_kernel, ts=ts, cdec=cdec),
        out_shape=jax.ShapeDtypeStruct((B * S, OW), BF16),
        grid=(B, nst),
        in_specs=[
            pl.BlockSpec((ts, HA), lambda b, s: (b * nst + s, 0)),
            pl.BlockSpec((ts, RET_DH), lambda b, s: (s, 0)),
            pl.BlockSpec((ts, RET_DH), lambda b, s: (s, 0)),
            full((RET_HEADS, CHUNK, CHUNK)), full((RET_HEADS, CHUNK, RET_DH)), full((RET_HEADS, CHUNK, RET_DH)),
            full(band.shape), full(shift.shape),
            full((len(POOL_WINDOWS), POOL_GW, POOL_GW)), full((1, POOL_W)),
            full((RG_CONV, RG_W)), full((1, RG_W)),
            full((RG_BLOCKS, RG_BD, RG_BD)), full((1, RG_W)),
            full((RG_BLOCKS, RG_BD, RG_BD)), full((1, RG_W)), full((1, RG_W)),
        ],
        out_specs=pl.BlockSpec((ts, OW), lambda b, s: (b * nst + s, 0)),
        scratch_shapes=[
            pltpu.VMEM((RET_HEADS, RET_DH, RET_DH), F32),
            pltpu.VMEM((1, RG_W), F32),
            pltpu.VMEM((2 * CHUNK, POOL_W + RG_W), BF16),
        ],
        compiler_params=_cparams(2),
        name="mixer_branches",
    )(ha, cos, sin, dmask, kdec, qdec, band, shift, pool_w, pool_scale, conv_w, conv_b, wa, ba, wx, bx, lam)


def _merge_kernel(br_ref, gl_ref, x_ref, wbr_ref, wout_ref, gb_ref, lng_ref, lnb_ref, o32_ref, o16_ref, *, alpha):
    D = x_ref.shape[1]
    acc = None
    for i in range(N_BRANCH):
        y = jnp.dot(br_ref[:, i * RET_W:(i + 1) * RET_W], wbr_ref[i], preferred_element_type=F32)
        gate = jax.nn.sigmoid(gl_ref[:, i * D:(i + 1) * D].astype(F32) + gb_ref[i:i + 1, :])
        acc = gate * y if acc is None else acc + gate * y
    m = jnp.dot(acc.astype(BF16), wout_ref[...], preferred_element_type=F32)
    z = _layer_norm(alpha * x_ref[...] + m, lng_ref[...], lnb_ref[...])
    o32_ref[...] = z
    o16_ref[...] = z.astype(BF16)


def _merge(br, gl, x32, w_branch, w_out, gate_b, ln_g, ln_b, alpha):
    T, D = x32.shape
    tm = min(MG_TM, T)
    full = lambda shape: pl.BlockSpec(shape, lambda i: (0,) * len(shape))
    return pl.pallas_call(
        functools.partial(_merge_kernel, alpha=alpha),
        out_shape=(jax.ShapeDtypeStruct((T, D), F32), jax.ShapeDtypeStruct((T, D), BF16)),
        grid=(T // tm,),
        in_specs=[
            pl.BlockSpec((tm, br.shape[1]), lambda i: (i, 0)),
            pl.BlockSpec((tm, gl.shape[1]), lambda i: (i, 0)),
            pl.BlockSpec((tm, D), lambda i: (i, 0)),
            full(w_branch.shape), full(w_out.shape), full(gate_b.shape), full((1, D)), full((1, D)),
        ],
        out_specs=(pl.BlockSpec((tm, D), lambda i: (i, 0)), pl.BlockSpec((tm, D), lambda i: (i, 0))),
        compiler_params=_cparams(1),
        name="mixer_merge",
    )(br, gl, x32, w_branch, w_out, gate_b, ln_g, ln_b)


def _first_argmax(v, iota, n):
    m = jnp.max(v, axis=0, keepdims=True)
    idx = jnp.min(jnp.where(v == m, iota, n), axis=0, keepdims=True)
    return m, idx


def _router_kernel(x_ref, rwt_ref, rb_ref, tri_ref, e_ref, g_ref, rk_ref, cnt_ref):
    E = rwt_ref.shape[0]
    nt = x_ref.shape[0]
    logits = lax.dot_general(rwt_ref[...], x_ref[...], (((1,), (1,)), ((), ())),
                             preferred_element_type=F32, precision=lax.Precision.HIGHEST)
    scores = jax.nn.sigmoid(logits)
    biased = scores + rb_ref[...]

    sub = lax.broadcasted_iota(I32, (E_PER_GROUP, nt), 0)
    giota = lax.broadcasted_iota(I32, (N_GROUPS, nt), 0)
    gs = jnp.zeros((N_GROUPS, nt), F32)
    for g in range(N_GROUPS):
        bg = biased[g * E_PER_GROUP:(g + 1) * E_PER_GROUP, :]
        m1, i1 = _first_argmax(bg, sub, E_PER_GROUP)
        m2 = jnp.max(jnp.where(sub == i1, NEG_INF, bg), axis=0, keepdims=True)
        gs = jnp.where(giota == g, m1 + m2, gs)
    gsel = jnp.zeros((N_GROUPS, nt), jnp.bool_)
    v = gs
    for _ in range(TOPK_GROUPS):
        _, idx = _first_argmax(v, giota, N_GROUPS)
        hit = giota == idx
        gsel = jnp.logical_or(gsel, hit)
        v = jnp.where(hit, NEG_INF, v)
    gself = jnp.where(gsel, 1.0, 0.0)
    masked = jnp.concatenate(
        [jnp.where(jnp.broadcast_to(gself[g:g + 1, :], (E_PER_GROUP, nt)) > 0.5,
                   biased[g * E_PER_GROUP:(g + 1) * E_PER_GROUP, :], NEG_INF) for g in range(N_GROUPS)], axis=0)

    eiota = lax.broadcasted_iota(I32, (E, nt), 0)
    v = masked
    onehot = jnp.zeros((E, nt), F32)
    idxs, sels = [], []
    for _ in range(TOP_K):
        _, idx = _first_argmax(v, eiota, E)
        hit = eiota == idx
        sels.append(jnp.sum(jnp.where(hit, scores, 0.0), axis=0, keepdims=True))
        idxs.append(idx)
        onehot = onehot + jnp.where(hit, 1.0, 0.0)
        v = jnp.where(hit, NEG_INF, v)
    ssum = sels[0]
    for k in range(1, TOP_K):
        ssum = ssum + sels[k]
    excl = jnp.dot(onehot.astype(BF16), tri_ref[...], preferred_element_type=F32)
    for k in range(TOP_K):
        e_ref[k:k + 1, :] = idxs[k]
        g_ref[k:k + 1, :] = sels[k] / ssum * ROUTED_SCALE
        rk_ref[k:k + 1, :] = jnp.sum(jnp.where(eiota == idxs[k], excl, 0.0), axis=0, keepdims=True).astype(I32)
    cnt_ref[...] = jnp.sum(onehot, axis=1, keepdims=True).astype(I32)


def _router(x32, rwt, rb, tri):
    T, D = x32.shape
    E = rwt.shape[0]
    J = T // SUB
    row = lambda dt: jax.ShapeDtypeStruct((TOP_K, T), dt)
    return pl.pallas_call(
        _router_kernel,
        out_shape=(row(I32), row(F32), row(I32), jax.ShapeDtypeStruct((J, E, 1), I32)),
        grid=(J,),
        in_specs=[
            pl.BlockSpec((SUB, D), lambda j: (j, 0)),
            pl.BlockSpec((E, D), lambda j: (0, 0)),
            pl.BlockSpec((E, 1), lambda j: (0, 0)),
            pl.BlockSpec((SUB, SUB), lambda j: (0, 0)),
        ],
        out_specs=(
            pl.BlockSpec((TOP_K, SUB), lambda j: (0, j)),
            pl.BlockSpec((TOP_K, SUB), lambda j: (0, j)),
            pl.BlockSpec((TOP_K, SUB), lambda j: (0, j)),
            pl.BlockSpec((None, E, 1), lambda j: (j, 0, 0)),
        ),
        compiler_params=_cparams(1),
        name="moe_router",
    )(x32, rwt, rb, tri)


def _round_up(a, m):
    return (a + m - 1) // m * m


def _local_rows_max(E):
    return _round_up(SUB * TOP_K + E * (ROW_ALIGN - 1) + LOCAL_ALIGN - ROW_ALIGN, PAIR)


def _num_blocks_max(T, E):
    J = T // SUB
    rows = T * TOP_K + J * (E * (ROW_ALIGN - 1) + LOCAL_ALIGN - ROW_ALIGN) + E * (FFN_BLK - ROW_ALIGN)
    return -(-rows // FFN_BLK)


def _tables(cnt, nblk_max, lmax):
    J, E = cnt.shape
    pc = _round_up(cnt, ROW_ALIGN)
    tot0 = jnp.sum(pc, axis=1)
    pc = pc.at[:, E - 1].add(_round_up(tot0, LOCAL_ALIGN) - tot0)
    lo = jnp.cumsum(pc, axis=1) - pc
    tot = jnp.sum(pc, axis=1)
    se = jnp.sum(pc, axis=0)
    reg = _round_up(se, FFN_BLK)
    creg = jnp.cumsum(reg)
    gs = creg - reg
    go = gs[None, :] + jnp.cumsum(pc, axis=0) - pc
    nused = (creg[-1] // FFN_BLK).reshape(1)
    bstart = jnp.arange(nblk_max, dtype=I32) * FFN_BLK
    bexp = jnp.minimum(jnp.sum(bstart[:, None] >= creg[None, :], axis=1), E - 1)
    trow = jnp.arange(lmax // ROW_ALIGN, dtype=I32) * ROW_ALIGN
    inrun = (trow[None, :, None] >= lo[:, None, :]) & (trow[None, :, None] < (lo + pc)[:, None, :])
    dst = trow[None, :] + jnp.sum(jnp.where(inrun, (go - lo)[:, None, :], 0), axis=2)
    i32 = lambda a: a.astype(I32)
    return dict(lo=i32(lo), dst=i32(dst), tot=i32(tot), tail_start=i32(gs + se), tail_len=i32(reg - se),
                nused=i32(nused), bexp=i32(bexp))


def _wait_rows(src_ref, dst_ref, sem, nrows):
    def wait_unit(unit):
        def body(i, c):
            pltpu.make_async_copy(src_ref.at[pl.ds(0, unit)], dst_ref.at[pl.ds(0, unit)], sem).wait()
            return c
        return body

    lax.fori_loop(0, nrows // SUB, wait_unit(SUB), 0)
    lax.fori_loop(0, (nrows % SUB) // LOCAL_ALIGN, wait_unit(LOCAL_ALIGN), 0)


def _start_tile_copies(copy_tile, dst_s, table_base, tot):
    def body(q, c):
        for u in range(ISSUE_UNROLL):
            i = q * ISSUE_UNROLL + u
            copy_tile(pl.multiple_of(i * ROW_ALIGN, ROW_ALIGN), pl.multiple_of(dst_s[table_base + i], ROW_ALIGN))
        return c

    lax.fori_loop(0, tot // LOCAL_ALIGN, body, 0)


def _dispatch_kernel(dst_s, tot_s, tls_s, tll_s, nu_s,
                     e_ref, rk_ref, lov_ref, x_ref, xs_ref, slot_ref, xbuf, zbuf, sem, *, E, J, LT, nblk_max):
    j = pl.program_id(0)
    par = j % 2
    nt = x_ref.shape[0]

    eiota = lax.broadcasted_iota(I32, (E, nt), 0)
    for k in range(TOP_K):
        base = jnp.sum(jnp.where(eiota == e_ref[k:k + 1, :], lov_ref[...], 0), axis=0, keepdims=True)
        slot_ref[k:k + 1, :] = base + rk_ref[k:k + 1, :]
    slots16 = slot_ref[...].astype(jnp.int16)
    xb = x_ref[...]
    tot = tot_s[j]
    buf = xbuf.at[par]

    def pair(c, carry):
        for u in range(PAIR // SUB):
            base = pl.multiple_of(c * PAIR + u * SUB, SUB)
            siota = (lax.broadcasted_iota(I32, (SUB, nt), 0) + base).astype(jnp.int16)
            pb = jnp.zeros((SUB, nt), BF16)
            for k in range(TOP_K):
                pb = jnp.where(siota == slots16[k:k + 1, :], jnp.ones((), BF16), pb)
            buf[pl.ds(base, SUB), :] = jnp.dot(pb, xb, preferred_element_type=F32).astype(BF16)
        return carry

    lax.fori_loop(0, (tot + PAIR - 1) // PAIR, pair, 0)

    def copy_tile(local_row, sorted_row):
        pltpu.make_async_copy(buf.at[pl.ds(local_row, ROW_ALIGN)], xs_ref.at[pl.ds(sorted_row, ROW_ALIGN)],
                              sem.at[par]).start()

    _start_tile_copies(copy_tile, dst_s, j * LT, tot)

    @pl.when(j == 0)
    def _():
        zbuf[...] = jnp.zeros_like(zbuf)
        ztile = zbuf.at[pl.ds(0, ROW_ALIGN)]

        def per_expert_tail(e, ntile):
            n = tll_s[e] // ROW_ALIGN

            def per_tile(i, c):
                do = pl.multiple_of(tls_s[e] + i * ROW_ALIGN, ROW_ALIGN)
                pltpu.make_async_copy(ztile, xs_ref.at[pl.ds(do, ROW_ALIGN)], sem.at[2]).start()
                return c

            lax.fori_loop(0, n, per_tile, 0)
            return ntile + n

        ntile = lax.fori_loop(0, E, per_expert_tail, 0)

        def per_block(b, c):
            do = pl.multiple_of(b * FFN_BLK, FFN_BLK)
            pltpu.make_async_copy(zbuf, xs_ref.at[pl.ds(do, FFN_BLK)], sem.at[2]).start()
            return c

        lax.fori_loop(nu_s[0], nblk_max, per_block, 0)

        def wtail(i, c):
            pltpu.make_async_copy(ztile, xs_ref.at[pl.ds(0, ROW_ALIGN)], sem.at[2]).wait()
            return c

        def wblock(b, c):
            pltpu.make_async_copy(zbuf, xs_ref.at[pl.ds(0, FFN_BLK)], sem.at[2]).wait()
            return c

        lax.fori_loop(0, ntile, wtail, 0)
        lax.fori_loop(nu_s[0], nblk_max, wblock, 0)

    @pl.when(j > 0)
    def _():
        _wait_rows(xbuf.at[1 - par], xs_ref, sem.at[1 - par], tot_s[j - 1])

    @pl.when(j == J - 1)
    def _():
        _wait_rows(buf, xs_ref, sem.at[par], tot)


def _dispatch(tb, top_e, rank, x16, nblk_max, lmax):
    T, D = x16.shape
    J, E = tb["lo"].shape
    gs = pltpu.PrefetchScalarGridSpec(
        num_scalar_prefetch=5,
        grid=(J,),
        in_specs=[
            pl.BlockSpec((TOP_K, SUB), lambda j, *_: (0, j)),
            pl.BlockSpec((TOP_K, SUB), lambda j, *_: (0, j)),
            pl.BlockSpec((None, E, 1), lambda j, *_: (j, 0, 0)),
            pl.BlockSpec((SUB, D), lambda j, *_: (j, 0)),
        ],
        out_specs=(pl.BlockSpec(memory_space=pl.ANY), pl.BlockSpec((TOP_K, SUB), lambda j, *_: (0, j))),
        scratch_shapes=[
            pltpu.VMEM((2, lmax, D), BF16),
            pltpu.VMEM((FFN_BLK, D), BF16),
            pltpu.SemaphoreType.DMA((3,)),
        ],
    )
    return pl.pallas_call(
        functools.partial(_dispatch_kernel, E=E, J=J, LT=lmax // ROW_ALIGN, nblk_max=nblk_max),
        out_shape=(jax.ShapeDtypeStruct((nblk_max * FFN_BLK, D), BF16), jax.ShapeDtypeStruct((TOP_K, T), I32)),
        grid_spec=gs,
        compiler_params=_cparams(1),
        name="moe_dispatch",
    )(tb["dst"].reshape(-1), tb["tot"], tb["tail_start"], tb["tail_len"], tb["nused"],
      top_e, rank, tb["lo"].reshape(J, E, 1), x16)


def _ffn_kernel(be_s, nu_s, x_ref, w1_ref, w3_ref, w2_ref, y_ref):
    @pl.when(pl.program_id(0) < nu_s[0])
    def _():
        xb = x_ref[...]
        h1 = jnp.dot(xb, w1_ref[...].astype(BF16), preferred_element_type=F32)
        h3 = jnp.dot(xb, w3_ref[...].astype(BF16), preferred_element_type=F32)
        hh = (jax.nn.silu(h1) * h3).astype(BF16)
        y_ref[...] = jnp.dot(hh, w2_ref[...].astype(BF16), preferred_element_type=F32).astype(BF16)


def _expert_ffn(tb, xs, w1, w3, w2, layer, nblk_max):
    D = xs.shape[1]
    blk = lambda b, be, nu: jnp.minimum(b, nu[0] - 1)
    wspec = lambda shape: pl.BlockSpec((None, None) + shape, lambda b, be, nu: (layer, be[blk(b, be, nu)], 0, 0))
    gs = pltpu.PrefetchScalarGridSpec(
        num_scalar_prefetch=2,
        grid=(nblk_max,),
        in_specs=[
            pl.BlockSpec((FFN_BLK, D), lambda b, be, nu: (blk(b, be, nu), 0)),
            wspec((D, D_EXPERT)), wspec((D, D_EXPERT)), wspec((D_EXPERT, D)),
        ],
        out_specs=pl.BlockSpec((FFN_BLK, D), lambda b, be, nu: (blk(b, be, nu), 0)),
    )
    return pl.pallas_call(
        _ffn_kernel,
        out_shape=jax.ShapeDtypeStruct(xs.shape, BF16),
        grid_spec=gs,
        input_output_aliases={2: 0},
        compiler_params=_cparams(1),
        name="moe_expert_ffn",
    )(tb["bexp"], tb["nused"], xs, w1, w3, w2)


def _combine_kernel(dst_s, tot_s,
                    slot_ref, g_ref, x32_ref, x16_ref, ys_ref, sw13_ref, sw2_ref, lng_ref, lnb_ref,
                    o32_ref, o16_ref, ybuf, acc_ref, sem, *, J, LT, alpha):
    j = pl.program_id(0)
    par = j % 2
    nt = x32_ref.shape[0]
    tot = tot_s[j]
    buf = ybuf.at[par]

    def start_gather(step, slot):
        def copy_tile(local_row, sorted_row):
            pltpu.make_async_copy(ys_ref.at[pl.ds(sorted_row, ROW_ALIGN)],
                                  ybuf.at[slot, pl.ds(local_row, ROW_ALIGN)], sem.at[slot]).start()

        _start_tile_copies(copy_tile, dst_s, step * LT, tot_s[step])

    @pl.when(j == 0)
    def _():
        start_gather(0, 0)

    @pl.when(j + 1 < J)
    def _():
        start_gather(j + 1, 1 - par)

    npair = (tot + PAIR - 1) // PAIR

    def zrows(i, c):
        zo = pl.multiple_of(tot + i * LOCAL_ALIGN, LOCAL_ALIGN)
        buf[pl.ds(zo, LOCAL_ALIGN), :] = jnp.zeros((LOCAL_ALIGN, ybuf.shape[2]), BF16)
        return c

    lax.fori_loop(0, (npair * PAIR - tot) // LOCAL_ALIGN, zrows, 0)

    xb = x16_ref[...]
    h = jnp.dot(xb, sw13_ref[...], preferred_element_type=F32)
    hh = (jax.nn.silu(h[:, :D_EXPERT]) * h[:, D_EXPERT:]).astype(BF16)
    acc_ref[...] = jnp.dot(hh, sw2_ref[...], preferred_element_type=F32)

    _wait_rows(ys_ref, buf, sem.at[par], tot)

    slots16 = slot_ref[...].astype(jnp.int16)
    gates16 = g_ref[...].astype(BF16)

    def pair(c, carry):
        part = None
        for u in range(PAIR // SUB):
            base = pl.multiple_of(c * PAIR + u * SUB, SUB)
            siota = (lax.broadcasted_iota(I32, (SUB, nt), 0) + base).astype(jnp.int16)
            wt = jnp.zeros((SUB, nt), BF16)
            for k in range(TOP_K):
                wt = jnp.where(siota == slots16[k:k + 1, :], gates16[k:k + 1, :], wt)
            y = lax.dot_general(wt, buf[pl.ds(base, SUB), :], (((0,), (0,)), ((), ())),
                                preferred_element_type=F32)
            part = y if part is None else part + y
        acc_ref[...] += part
        return carry

    lax.fori_loop(0, npair, pair, 0)

    z = _layer_norm(alpha * x32_ref[...] + acc_ref[...], lng_ref[...], lnb_ref[...])
    o32_ref[...] = z
    o16_ref[...] = z.astype(BF16)


def _combine(tb, slot, gate, x32, x16, ys, sw13, sw2, ln_g, ln_b, alpha, lmax):
    T, D = x32.shape
    J = T // SUB
    full = lambda shape: pl.BlockSpec(shape, lambda j, *_: (0,) * len(shape))
    gs = pltpu.PrefetchScalarGridSpec(
        num_scalar_prefetch=2,
        grid=(J,),
        in_specs=[
            pl.BlockSpec((TOP_K, SUB), lambda j, *_: (0, j)),
            pl.BlockSpec((TOP_K, SUB), lambda j, *_: (0, j)),
            pl.BlockSpec((SUB, D), lambda j, *_: (j, 0)),
            pl.BlockSpec((SUB, D), lambda j, *_: (j, 0)),
            pl.BlockSpec(memory_space=pl.ANY),
            full(sw13.shape), full(sw2.shape), full((1, D)), full((1, D)),
        ],
        out_specs=(pl.BlockSpec((SUB, D), lambda j, *_: (j, 0)), pl.BlockSpec((SUB, D), lambda j, *_: (j, 0))),
        scratch_shapes=[
            pltpu.VMEM((2, lmax, D), BF16),
            pltpu.VMEM((SUB, D), F32),
            pltpu.SemaphoreType.DMA((2,)),
        ],
    )
    return pl.pallas_call(
        functools.partial(_combine_kernel, J=J, LT=lmax // ROW_ALIGN, alpha=alpha),
        out_shape=(jax.ShapeDtypeStruct((T, D), F32), jax.ShapeDtypeStruct((T, D), BF16)),
        grid_spec=gs,
        compiler_params=_cparams(1),
        name="moe_combine",
    )(tb["dst"].reshape(-1), tb["tot"], slot, gate, x32, x16, ys, sw13, sw2, ln_g, ln_b)


def kernel(x, positions, w_in, gate_b, w_branch, w_out, pool_w, pool_scale, rg_conv_w, rg_conv_b, rg_wa, rg_ba,
           rg_wx, rg_bx, rg_lambda, ln1_g, ln1_b, router_w, router_bias, exp_w1, exp_w3, exp_w2, sh_w1, sh_w3,
           sh_w2, ln2_g, ln2_b):
    B, S, D = x.shape
    L = w_in.shape[0]
    T = B * S
    E = router_w.shape[2]
    assert T % SUB == 0 and S % CHUNK == 0 and E == N_GROUPS * E_PER_GROUP
    alpha = (2 * L) ** 0.25
    n_a = 4 * RET_W + POOL_W + 2 * RG_W
    nblk_max = _num_blocks_max(T, E)
    lmax = _local_rows_max(E)

    cos, sin = _rope_tables(positions)
    consts = _retention_consts()
    tri = (jnp.arange(SUB)[:, None] < jnp.arange(SUB)[None, :]).astype(BF16)
    row = lambda a: a.reshape(1, -1)

    x32 = x.reshape(T, D)
    x16 = x32.astype(BF16)
    for l in range(L):
        ha = _in_proj(x16, w_in, l, 0, n_a, "in_proj_a")
        gl = _in_proj(x16, w_in, l, n_a, w_in.shape[2] - n_a, "in_proj_gates")
        br = _branches(ha, cos, sin, consts, pool_w[l].astype(BF16), row(pool_scale[l]), rg_conv_w[l],
                       row(rg_conv_b[l]), rg_wa[l].astype(BF16), row(rg_ba[l]), rg_wx[l].astype(BF16),
                       row(rg_bx[l]), row(rg_lambda[l]), B, S)
        x32, x16 = _merge(br, gl, x32, w_branch[l].astype(BF16), w_out[l].astype(BF16), gate_b[l],
                          row(ln1_g[l]), row(ln1_b[l]), alpha)

        top_e, gate, rank, cnt = _router(x32, router_w[l].T, router_bias[l].reshape(E, 1), tri)
        tb = _tables(cnt[:, :, 0], nblk_max, lmax)
        xs, slot = _dispatch(tb, top_e, rank, x16, nblk_max, lmax)
        ys = _expert_ffn(tb, xs, exp_w1, exp_w3, exp_w2, l, nblk_max)
        sw13 = jnp.concatenate([sh_w1[l], sh_w3[l]], axis=-1).astype(BF16)
        x32, x16 = _combine(tb, slot, gate, x32, x16, ys, sw13, sh_w2[l].astype(BF16),
                            row(ln2_g[l]), row(ln2_b[l]), alpha, lmax)
    return x32.reshape(B, S, D)
```

```python
import functools
import math

import jax
import jax.numpy as jnp
from jax import lax
from jax.experimental import pallas as pl
from jax.experimental.pallas import tpu as pltpu

F32, BF16, I32 = jnp.float32, jnp.bfloat16, jnp.int32

RET_HEADS, RET_DH, RET_W, CHUNK = 4, 128, 512, 128
ROPE_BASE = 10000.0
POOL_WINDOWS, POOL_GW, POOL_W = (2, 4, 8, 16), 128, 512
RG_BLOCKS, RG_BD, RG_W, RG_CONV, RG_C = 4, 128, 512, 4, 8.0
HIST = 16
N_BRANCH = 3
N_GROUPS, E_PER_GROUP, TOPK_GROUPS, TOP_K, D_EXPERT = 8, 8, 4, 8, 256
ROUTED_SCALE = 2.5
LN_EPS, GN_EPS = 1e-5, 1e-6

VMEM_LIMIT = 56 * 1024 * 1024
MM_TM, MM_TN = 2048, 512
BR_TS = 512
MG_TM = 512
SUB = 256
ROW_ALIGN = 16
ISSUE_UNROLL = 4
LOCAL_ALIGN = ROW_ALIGN * ISSUE_UNROLL
PAIR = 2 * SUB
FFN_BLK = 1024
NEG_INF = float("-inf")


def _cparams(n_axes):
    return pltpu.CompilerParams(dimension_semantics=("arbitrary",) * n_axes, vmem_limit_bytes=VMEM_LIMIT)


def _layer_norm(z, g, b):
    mu = jnp.mean(z, axis=-1, keepdims=True)
    zc = z - mu
    var = jnp.mean(zc * zc, axis=-1, keepdims=True)
    return zc * lax.rsqrt(var + LN_EPS) * g + b


def _rope_kernel(pos_ref, inv_ref, sign_ref, cos_ref, sin_ref):
    ang = pos_ref[...].astype(F32) * inv_ref[...]
    cos_ref[...] = jnp.cos(ang)
    sin_ref[...] = jnp.sin(ang) * sign_ref[...]


def _rope_tables(positions):
    S = positions.shape[0]
    half = RET_DH // 2
    inv = ROPE_BASE ** (-jnp.arange(half, dtype=F32) / half)
    inv2 = jnp.concatenate([inv, inv]).reshape(1, RET_DH)
    sign = jnp.concatenate([-jnp.ones((half,), F32), jnp.ones((half,), F32)]).reshape(1, RET_DH)
    return pl.pallas_call(
        _rope_kernel,
        out_shape=(jax.ShapeDtypeStruct((S, RET_DH), F32), jax.ShapeDtypeStruct((S, RET_DH), F32)),
        name="rope_tables",
    )(positions.reshape(S, 1), inv2, sign)


def _mm_kernel(x_ref, w_ref, o_ref):
    o_ref[...] = jnp.dot(x_ref[...], w_ref[...].astype(BF16), preferred_element_type=F32).astype(o_ref.dtype)


def _in_proj(x, w_all, layer, col0, ncols, name):
    M, K = x.shape
    tm, tn = min(MM_TM, M), MM_TN
    assert col0 % tn == 0 and ncols % tn == 0
    return pl.pallas_call(
        _mm_kernel,
        out_shape=jax.ShapeDtypeStruct((M, ncols), BF16),
        grid=(M // tm, ncols // tn),
        in_specs=[pl.BlockSpec((tm, K), lambda i, j: (i, 0)),
                  pl.BlockSpec((None, K, tn), lambda i, j: (layer, 0, j + col0 // tn))],
        out_specs=pl.BlockSpec((tm, tn), lambda i, j: (i, j)),
        compiler_params=_cparams(2),
        name=name,
    )(x, w_all)


def _log_sigmoid(x):
    return jnp.minimum(x, 0.0) - jnp.log(1.0 + jnp.exp(-jnp.abs(x)))


def _branch_kernel(ha_ref, cos_ref, sin_ref, dmask_ref, kdec_ref, qdec_ref, band_ref, shift_ref, poolw_ref,
                   pscale_ref, convw_ref, convb_ref, wa_ref, ba_ref, wx_ref, bx_ref, lam_ref,
                   o_ref, state_ref, hcar_ref, hist_ref, *, ts, cdec):
    s = pl.program_id(1)
    x0 = 4 * RET_W

    @pl.when(s == 0)
    def _():
        state_ref[...] = jnp.zeros_like(state_ref)
        hcar_ref[...] = jnp.zeros_like(hcar_ref)
        hist_ref[...] = jnp.zeros_like(hist_ref)

    log_sig_lam = _log_sigmoid(lam_ref[...])

    def chunk(c, carry):
        r0 = pl.multiple_of(c * CHUNK, CHUNK)
        rows = pl.ds(r0, CHUNK)
        cosf = cos_ref[rows, :]
        sinf = sin_ref[rows, :]

        hist_ref[CHUNK:2 * CHUNK, :] = ha_ref[rows, slice(x0, x0 + POOL_W + RG_W)]
        wins = [jnp.dot(band_ref[gi], hist_ref[:, gi * POOL_GW:(gi + 1) * POOL_GW], preferred_element_type=F32)
                for gi in range(len(POOL_WINDOWS))]
        lagged = [jnp.dot(shift_ref[kk - 1], hist_ref[:, POOL_W:POOL_W + RG_W], preferred_element_type=F32)
                  for kk in range(1, RG_CONV)]
        hist_ref[CHUNK - HIST:CHUNK, :] = hist_ref[2 * CHUNK - HIST:2 * CHUNK, :]

        def retention_head(h):
            cs = slice(h * RET_DH, (h + 1) * RET_DH)
            q = ha_ref[rows, cs].astype(F32)
            k = ha_ref[rows, slice(RET_W + h * RET_DH, RET_W + (h + 1) * RET_DH)].astype(F32)
            v = ha_ref[rows, slice(2 * RET_W + h * RET_DH, 2 * RET_W + (h + 1) * RET_DH)]
            g = ha_ref[rows, slice(3 * RET_W + h * RET_DH, 3 * RET_W + (h + 1) * RET_DH)].astype(F32)
            qr = q * cosf + pltpu.roll(q, RET_DH // 2, 1) * sinf
            kr = (k * cosf + pltpu.roll(k, RET_DH // 2, 1) * sinf) * (RET_DH ** -0.5)
            qb = qr.astype(BF16)
            sc = lax.dot_general(qb, kr.astype(BF16), (((1,), (1,)), ((), ())),
                                 preferred_element_type=F32) * dmask_ref[h]
            o = jnp.dot(sc.astype(BF16), v, preferred_element_type=F32)
            st = state_ref[h]
            o = o + jnp.dot(qb, st.astype(BF16), preferred_element_type=F32) * qdec_ref[h]
            kd = (kr * kdec_ref[h]).astype(BF16)
            state_ref[h] = st * cdec[h] + lax.dot_general(kd, v, (((0,), (0,)), ((), ())),
                                                          preferred_element_type=F32)
            mu = jnp.mean(o, axis=-1, keepdims=True)
            oc = o - mu
            var = jnp.mean(oc * oc, axis=-1, keepdims=True)
            o_ref[rows, cs] = (jax.nn.silu(g) * (oc * lax.rsqrt(var + GN_EPS))).astype(BF16)

        retention_head(0)

        xr = ha_ref[rows, slice(x0 + POOL_W, x0 + POOL_W + RG_W)].astype(F32)
        conv = convb_ref[...] + xr * convw_ref[0:1, :]
        for kk in range(1, RG_CONV):
            conv = conv + lagged[kk - 1] * convw_ref[kk:kk + 1, :]
        cb = conv.astype(BF16)
        rl = jnp.concatenate([jnp.dot(cb[:, n * RG_BD:(n + 1) * RG_BD], wa_ref[n], preferred_element_type=F32)
                              for n in range(RG_BLOCKS)], axis=1)
        il = jnp.concatenate([jnp.dot(cb[:, n * RG_BD:(n + 1) * RG_BD], wx_ref[n], preferred_element_type=F32)
                              for n in range(RG_BLOCKS)], axis=1)

        retention_head(1)

        r = jax.nn.sigmoid(rl + ba_ref[...])
        ig = jax.nn.sigmoid(il + bx_ref[...])
        log_a = RG_C * r * log_sig_lam
        a = jnp.exp(log_a)
        om = -jnp.tanh(log_a) * (a * a + 1.0)
        bb = jnp.where(om > 0.0, om * lax.rsqrt(om), 0.0) * (ig * conv)

        retention_head(2)

        row = lax.broadcasted_iota(I32, (CHUNK, RG_W), 0)
        A, Bv = a, bb
        d = 1
        while d < CHUNK:
            As = jnp.where(row < d, 1.0, pltpu.roll(A, d, 0))
            Bs = jnp.where(row < d, 0.0, pltpu.roll(Bv, d, 0))
            Bv = A * Bs + Bv
            A = A * As
            d *= 2

        retention_head(3)

        hh = A * hcar_ref[...] + Bv
        hcar_ref[...] = hh[CHUNK - 1:CHUNK, :]
        yr = ha_ref[rows, slice(x0 + POOL_W + RG_W, x0 + POOL_W + 2 * RG_W)].astype(F32)
        o_ref[rows, slice(RET_W + POOL_W, RET_W + POOL_W + RG_W)] = (hh * jax.nn.gelu(yr)).astype(BF16)

        xp = ha_ref[rows, slice(x0, x0 + POOL_W)].astype(F32)
        t_idx = s * ts + r0 + lax.broadcasted_iota(I32, (CHUNK, POOL_GW), 0)
        for gi, w in enumerate(POOL_WINDOWS):
            cs = slice(gi * POOL_GW, (gi + 1) * POOL_GW)
            n = jnp.minimum(w, t_idx + 1).astype(F32)
            pooled = wins[gi] / n - xp[:, cs]
            y = jnp.dot(pooled.astype(BF16), poolw_ref[gi], preferred_element_type=F32) * pscale_ref[:, cs]
            o_ref[rows, slice(RET_W + gi * POOL_GW, RET_W + (gi + 1) * POOL_GW)] = y.astype(BF16)
        return carry

    lax.fori_loop(0, ts // CHUNK, chunk, 0, unroll=2)


def _retention_consts():
    log_g = [math.log(1.0 - 2.0 ** (-5.0 - h)) for h in range(RET_HEADS)]
    lg = jnp.asarray(log_g, F32)
    idx = jnp.arange(CHUNK, dtype=F32)
    diff = idx[:, None] - idx[None, :]
    dmask = jnp.where(diff >= 0, jnp.exp(lg[:, None, None] * jnp.maximum(diff, 0.0)), 0.0)
    kdec = jnp.exp(lg[:, None] * (CHUNK - 1 - idx)[None, :])
    qdec = jnp.exp(lg[:, None] * (idx + 1.0)[None, :])
    kdec = jnp.broadcast_to(kdec[:, :, None], (RET_HEADS, CHUNK, RET_DH))
    qdec = jnp.broadcast_to(qdec[:, :, None], (RET_HEADS, CHUNK, RET_DH))
    cdec = tuple(math.exp(v * CHUNK) for v in log_g)
    t = jnp.arange(CHUNK)[:, None]
    c = jnp.arange(2 * CHUNK)[None, :]
    lag = CHUNK + t - c
    band = jnp.stack([(lag >= 0) & (lag < w) for w in POOL_WINDOWS]).astype(BF16)
    shift = jnp.stack([lag == k for k in range(1, RG_CONV)]).astype(BF16)
    return dmask, kdec, qdec, band, shift, cdec


def _branches(ha, cos, sin, consts, pool_w, pool_scale, conv_w, conv_b, wa, ba, wx, bx, lam, B, S):
    dmask, kdec, qdec, band, shift, cdec = consts
    ts = min(BR_TS, S)
    nst = S // ts
    HA = ha.shape[1]
    OW = RET_W + POOL_W + RG_W
    full = lambda shape: pl.BlockSpec(shape, lambda b, s: (0,) * len(shape))
    return pl.pallas_call(
        functools.partial(_branch_kernel, ts=ts, cdec=cdec),
        out_shape=jax.ShapeDtypeStruct((B * S, OW), BF16),
        grid=(B, nst),
        in_specs=[
            pl.BlockSpec((ts, HA), lambda b, s: (b * nst + s, 0)),
            pl.BlockSpec((ts, RET_DH), lambda b, s: (s, 0)),
            pl.BlockSpec((ts, RET_DH), lambda b, s: (s, 0)),
            full((RET_HEADS, CHUNK, CHUNK)), full((RET_HEADS, CHUNK, RET_DH)), full((RET_HEADS, CHUNK, RET_DH)),
            full(band.shape), full(shift.shape),
            full((len(POOL_WINDOWS), POOL_GW, POOL_GW)), full((1, POOL_W)),
            full((RG_CONV, RG_W)), full((1, RG_W)),
            full((RG_BLOCKS, RG_BD, RG_BD)), full((1, RG_W)),
            full((RG_BLOCKS, RG_BD, RG_BD)), full((1, RG_W)), full((1, RG_W)),
        ],
        out_specs=pl.BlockSpec((ts, OW), lambda b, s: (b * nst + s, 0)),
        scratch_shapes=[
            pltpu.VMEM((RET_HEADS, RET_DH, RET_DH), F32),
            pltpu.VMEM((1, RG_W), F32),
            pltpu.VMEM((2 * CHUNK, POOL_W + RG_W), BF16),
        ],
        compiler_params=_cparams(2),
        name="mixer_branches",
    )(ha, cos, sin, dmask, kdec, qdec, band, shift, pool_w, pool_scale, conv_w, conv_b, wa, ba, wx, bx, lam)


def _merge_kernel(br_ref, gl_ref, x_ref, wbr_ref, wout_ref, gb_ref, lng_ref, lnb_ref, o32_ref, o16_ref, *, alpha):
    D = x_ref.shape[1]
    acc = None
    for i in range(N_BRANCH):
        y = jnp.dot(br_ref[:, i * RET_W:(i + 1) * RET_W], wbr_ref[i], preferred_element_type=F32)
        gate = jax.nn.sigmoid(gl_ref[:, i * D:(i + 1) * D].astype(F32) + gb_ref[i:i + 1, :])
        acc = gate * y if acc is None else acc + gate * y
    m = jnp.dot(acc.astype(BF16), wout_ref[...], preferred_element_type=F32)
    z = _layer_norm(alpha * x_ref[...] + m, lng_ref[...], lnb_ref[...])
    o32_ref[...] = z
    o16_ref[...] = z.astype(BF16)


def _merge(br, gl, x32, w_branch, w_out, gate_b, ln_g, ln_b, alpha):
    T, D = x32.shape
    tm = min(MG_TM, T)
    full = lambda shape: pl.BlockSpec(shape, lambda i: (0,) * len(shape))
    return pl.pallas_call(
        functools.partial(_merge_kernel, alpha=alpha),
        out_shape=(jax.ShapeDtypeStruct((T, D), F32), jax.ShapeDtypeStruct((T, D), BF16)),
        grid=(T // tm,),
        in_specs=[
            pl.BlockSpec((tm, br.shape[1]), lambda i: (i, 0)),
            pl.BlockSpec((tm, gl.shape[1]), lambda i: (i, 0)),
            pl.BlockSpec((tm, D), lambda i: (i, 0)),
            full(w_branch.shape), full(w_out.shape), full(gate_b.shape), full((1, D)), full((1, D)),
        ],
        out_specs=(pl.BlockSpec((tm, D), lambda i: (i, 0)), pl.BlockSpec((tm, D), lambda i: (i, 0))),
        compiler_params=_cparams(1),
        name="mixer_merge",
    )(br, gl, x32, w_branch, w_out, gate_b, ln_g, ln_b)


def _first_argmax(v, iota, n):
    m = jnp.max(v, axis=0, keepdims=True)
    idx = jnp.min(jnp.where(v == m, iota, n), axis=0, keepdims=True)
    return m, idx


def _router_kernel(x_ref, rwt_ref, rb_ref, tri_ref, e_ref, g_ref, rk_ref, cnt_ref):
    E = rwt_ref.shape[0]
    nt = x_ref.shape[0]
    logits = lax.dot_general(rwt_ref[...], x_ref[...], (((1,), (1,)), ((), ())),
                             preferred_element_type=F32, precision=lax.Precision.HIGHEST)
    scores = jax.nn.sigmoid(logits)
    biased = scores + rb_ref[...]

    sub = lax.broadcasted_iota(I32, (E_PER_GROUP, nt), 0)
    giota = lax.broadcasted_iota(I32, (N_GROUPS, nt), 0)
    gs = jnp.zeros((N_GROUPS, nt), F32)
    for g in range(N_GROUPS):
        bg = biased[g * E_PER_GROUP:(g + 1) * E_PER_GROUP, :]
        m1, i1 = _first_argmax(bg, sub, E_PER_GROUP)
        m2 = jnp.max(jnp.where(sub == i1, NEG_INF, bg), axis=0, keepdims=True)
        gs = jnp.where(giota == g, m1 + m2, gs)
    gsel = jnp.zeros((N_GROUPS, nt), jnp.bool_)
    v = gs
    for _ in range(TOPK_GROUPS):
        _, idx = _first_argmax(v, giota, N_GROUPS)
        hit = giota == idx
        gsel = jnp.logical_or(gsel, hit)
        v = jnp.where(hit, NEG_INF, v)
    gself = jnp.where(gsel, 1.0, 0.0)
    masked = jnp.concatenate(
        [jnp.where(jnp.broadcast_to(gself[g:g + 1, :], (E_PER_GROUP, nt)) > 0.5,
                   biased[g * E_PER_GROUP:(g + 1) * E_PER_GROUP, :], NEG_INF) for g in range(N_GROUPS)], axis=0)

    eiota = lax.broadcasted_iota(I32, (E, nt), 0)
    v = masked
    onehot = jnp.zeros((E, nt), F32)
    idxs, sels = [], []
    for _ in range(TOP_K):
        _, idx = _first_argmax(v, eiota, E)
        hit = eiota == idx
        sels.append(jnp.sum(jnp.where(hit, scores, 0.0), axis=0, keepdims=True))
        idxs.append(idx)
        onehot = onehot + jnp.where(hit, 1.0, 0.0)
        v = jnp.where(hit, NEG_INF, v)
    ssum = sels[0]
    for k in range(1, TOP_K):
        ssum = ssum + sels[k]
    excl = jnp.dot(onehot.astype(BF16), tri_ref[...], preferred_element_type=F32)
    for k in range(TOP_K):
        e_ref[k:k + 1, :] = idxs[k]
        g_ref[k:k + 1, :] = sels[k] / ssum * ROUTED_SCALE
        rk_ref[k:k + 1, :] = jnp.sum(jnp.where(eiota == idxs[k], excl, 0.0), axis=0, keepdims=True).astype(I32)
    cnt_ref[...] = jnp.sum(onehot, axis=1, keepdims=True).astype(I32)


def _router(x32, rwt, rb, tri):
    T, D = x32.shape
    E = rwt.shape[0]
    J = T // SUB
    row = lambda dt: jax.ShapeDtypeStruct((TOP_K, T), dt)
    return pl.pallas_call(
        _router_kernel,
        out_shape=(row(I32), row(F32), row(I32), jax.ShapeDtypeStruct((J, E, 1), I32)),
        grid=(J,),
        in_specs=[
            pl.BlockSpec((SUB, D), lambda j: (j, 0)),
            pl.BlockSpec((E, D), lambda j: (0, 0)),
            pl.BlockSpec((E, 1), lambda j: (0, 0)),
            pl.BlockSpec((SUB, SUB), lambda j: (0, 0)),
        ],
        out_specs=(
            pl.BlockSpec((TOP_K, SUB), lambda j: (0, j)),
            pl.BlockSpec((TOP_K, SUB), lambda j: (0, j)),
            pl.BlockSpec((TOP_K, SUB), lambda j: (0, j)),
            pl.BlockSpec((None, E, 1), lambda j: (j, 0, 0)),
        ),
        compiler_params=_cparams(1),
        name="moe_router",
    )(x32, rwt, rb, tri)


def _round_up(a, m):
    return (a + m - 1) // m * m


def _local_rows_max(E):
    return _round_up(SUB * TOP_K + E * (ROW_ALIGN - 1) + LOCAL_ALIGN - ROW_ALIGN, PAIR)


def _num_blocks_max(T, E):
    J = T // SUB
    rows = T * TOP_K + J * (E * (ROW_ALIGN - 1) + LOCAL_ALIGN - ROW_ALIGN) + E * (FFN_BLK - ROW_ALIGN)
    return -(-rows // FFN_BLK)


def _tables(cnt, nblk_max, lmax):
    J, E = cnt.shape
    pc = _round_up(cnt, ROW_ALIGN)
    tot0 = jnp.sum(pc, axis=1)
    pc = pc.at[:, E - 1].add(_round_up(tot0, LOCAL_ALIGN) - tot0)
    lo = jnp.cumsum(pc, axis=1) - pc
    tot = jnp.sum(pc, axis=1)
    se = jnp.sum(pc, axis=0)
    reg = _round_up(se, FFN_BLK)
    creg = jnp.cumsum(reg)
    gs = creg - reg
    go = gs[None, :] + jnp.cumsum(pc, axis=0) - pc
    nused = (creg[-1] // FFN_BLK).reshape(1)
    bstart = jnp.arange(nblk_max, dtype=I32) * FFN_BLK
    bexp = jnp.minimum(jnp.sum(bstart[:, None] >= creg[None, :], axis=1), E - 1)
    trow = jnp.arange(lmax // ROW_ALIGN, dtype=I32) * ROW_ALIGN
    inrun = (trow[None, :, None] >= lo[:, None, :]) & (trow[None, :, None] < (lo + pc)[:, None, :])
    dst = trow[None, :] + jnp.sum(jnp.where(inrun, (go - lo)[:, None, :], 0), axis=2)
    i32 = lambda a: a.astype(I32)
    return dict(lo=i32(lo), dst=i32(dst), tot=i32(tot), tail_start=i32(gs + se), tail_len=i32(reg - se),
                nused=i32(nused), bexp=i32(bexp))


def _wait_rows(src_ref, dst_ref, sem, nrows):
    def wait_unit(unit):
        def body(i, c):
            pltpu.make_async_copy(src_ref.at[pl.ds(0, unit)], dst_ref.at[pl.ds(0, unit)], sem).wait()
            return c
        return body

    lax.fori_loop(0, nrows // SUB, wait_unit(SUB), 0)
    lax.fori_loop(0, (nrows % SUB) // LOCAL_ALIGN, wait_unit(LOCAL_ALIGN), 0)


def _start_tile_copies(copy_tile, dst_s, table_base, first_tile, rows):
    def body(q, c):
        for u in range(ISSUE_UNROLL):
            i = first_tile + q * ISSUE_UNROLL + u
            copy_tile(pl.multiple_of(i * ROW_ALIGN, ROW_ALIGN), pl.multiple_of(dst_s[table_base + i], ROW_ALIGN))
        return c

    lax.fori_loop(0, rows // LOCAL_ALIGN, body, 0)


def _dispatch_kernel(dst_s, tot_s, tls_s, tll_s, nu_s,
                     e_ref, rk_ref, lov_ref, x_ref, xs_ref, slot_ref, xbuf, zbuf, sem, *, E, J, LT, nblk_max):
    j = pl.program_id(0)
    par = j % 2
    nt = x_ref.shape[0]

    eiota = lax.broadcasted_iota(I32, (E, nt), 0)
    for k in range(TOP_K):
        base = jnp.sum(jnp.where(eiota == e_ref[k:k + 1, :], lov_ref[...], 0), axis=0, keepdims=True)
        slot_ref[k:k + 1, :] = base + rk_ref[k:k + 1, :]
    slots16 = slot_ref[...].astype(jnp.int16)
    xb = x_ref[...]
    tot = tot_s[j]
    buf = xbuf.at[par]

    def permute_pair(c):
        for u in range(PAIR // SUB):
            base = pl.multiple_of(c * PAIR + u * SUB, SUB)
            siota = (lax.broadcasted_iota(I32, (SUB, nt), 0) + base).astype(jnp.int16)
            pb = jnp.zeros((SUB, nt), BF16)
            for k in range(TOP_K):
                pb = jnp.where(siota == slots16[k:k + 1, :], jnp.ones((), BF16), pb)
            buf[pl.ds(base, SUB), :] = jnp.dot(pb, xb, preferred_element_type=F32).astype(BF16)

    def copy_tile(local_row, sorted_row):
        pltpu.make_async_copy(buf.at[pl.ds(local_row, ROW_ALIGN)], xs_ref.at[pl.ds(sorted_row, ROW_ALIGN)],
                              sem.at[par]).start()

    tiles_per_pair = PAIR // ROW_ALIGN

    def trip(c, carry):
        for i in range(tiles_per_pair):
            t = (c - 1) * tiles_per_pair + i
            copy_tile(pl.multiple_of(t * ROW_ALIGN, ROW_ALIGN), pl.multiple_of(dst_s[j * LT + t], ROW_ALIGN))
        permute_pair(c)
        return carry

    npair = (tot + PAIR - 1) // PAIR
    permute_pair(0)
    lax.fori_loop(1, npair, trip, 0)
    _start_tile_copies(copy_tile, dst_s, j * LT, (npair - 1) * tiles_per_pair, tot - (npair - 1) * PAIR)

    ztile = zbuf.at[pl.ds(0, ROW_ALIGN)]

    @pl.when(j == 0)
    def _():
        zbuf[...] = jnp.zeros_like(zbuf)

        def per_expert_tail(e, carry):
            def per_tile(i, c):
                do = pl.multiple_of(tls_s[e] + i * ROW_ALIGN, ROW_ALIGN)
                pltpu.make_async_copy(ztile, xs_ref.at[pl.ds(do, ROW_ALIGN)], sem.at[2]).start()
                return c

            lax.fori_loop(0, tll_s[e] // ROW_ALIGN, per_tile, 0)
            return carry

        lax.fori_loop(0, E, per_expert_tail, 0)

        def per_block(b, c):
            do = pl.multiple_of(b * FFN_BLK, FFN_BLK)
            pltpu.make_async_copy(zbuf, xs_ref.at[pl.ds(do, FFN_BLK)], sem.at[2]).start()
            return c

        lax.fori_loop(nu_s[0], nblk_max, per_block, 0)

    @pl.when(j > 0)
    def _():
        _wait_rows(xbuf.at[1 - par], xs_ref, sem.at[1 - par], tot_s[j - 1])

    @pl.when(j == J - 1)
    def _():
        _wait_rows(buf, xs_ref, sem.at[par], tot)

        def wtail(i, c):
            pltpu.make_async_copy(ztile, xs_ref.at[pl.ds(0, ROW_ALIGN)], sem.at[2]).wait()
            return c

        def wblock(b, c):
            pltpu.make_async_copy(zbuf, xs_ref.at[pl.ds(0, FFN_BLK)], sem.at[2]).wait()
            return c

        ntile = lax.fori_loop(0, E, lambda e, n: n + tll_s[e] // ROW_ALIGN, 0)
        lax.fori_loop(0, ntile, wtail, 0)
        lax.fori_loop(nu_s[0], nblk_max, wblock, 0)


def _dispatch(tb, top_e, rank, x16, nblk_max, lmax):
    T, D = x16.shape
    J, E = tb["lo"].shape
    gs = pltpu.PrefetchScalarGridSpec(
        num_scalar_prefetch=5,
        grid=(J,),
        in_specs=[
            pl.BlockSpec((TOP_K, SUB), lambda j, *_: (0, j)),
            pl.BlockSpec((TOP_K, SUB), lambda j, *_: (0, j)),
            pl.BlockSpec((None, E, 1), lambda j, *_: (j, 0, 0)),
            pl.BlockSpec((SUB, D), lambda j, *_: (j, 0)),
        ],
        out_specs=(pl.BlockSpec(memory_space=pl.ANY), pl.BlockSpec((TOP_K, SUB), lambda j, *_: (0, j))),
        scratch_shapes=[
            pltpu.VMEM((2, lmax, D), BF16),
            pltpu.VMEM((FFN_BLK, D), BF16),
            pltpu.SemaphoreType.DMA((3,)),
        ],
    )
    return pl.pallas_call(
        functools.partial(_dispatch_kernel, E=E, J=J, LT=lmax // ROW_ALIGN, nblk_max=nblk_max),
        out_shape=(jax.ShapeDtypeStruct((nblk_max * FFN_BLK, D), BF16), jax.ShapeDtypeStruct((TOP_K, T), I32)),
        grid_spec=gs,
        compiler_params=_cparams(1),
        name="moe_dispatch",
    )(tb["dst"].reshape(-1), tb["tot"], tb["tail_start"], tb["tail_len"], tb["nused"],
      top_e, rank, tb["lo"].reshape(J, E, 1), x16)


def _ffn_kernel(be_s, nu_s, x_ref, w1_ref, w3_ref, w2_ref, y_ref):
    @pl.when(pl.program_id(0) < nu_s[0])
    def _():
        xb = x_ref[...]
        h1 = jnp.dot(xb, w1_ref[...].astype(BF16), preferred_element_type=F32)
        h3 = jnp.dot(xb, w3_ref[...].astype(BF16), preferred_element_type=F32)
        hh = (jax.nn.silu(h1) * h3).astype(BF16)
        y_ref[...] = jnp.dot(hh, w2_ref[...].astype(BF16), preferred_element_type=F32).astype(BF16)


def _expert_ffn(tb, xs, w1, w3, w2, layer, nblk_max):
    D = xs.shape[1]
    blk = lambda b, be, nu: jnp.minimum(b, nu[0] - 1)
    wspec = lambda shape: pl.BlockSpec((None, None) + shape, lambda b, be, nu: (layer, be[blk(b, be, nu)], 0, 0))
    gs = pltpu.PrefetchScalarGridSpec(
        num_scalar_prefetch=2,
        grid=(nblk_max,),
        in_specs=[
            pl.BlockSpec((FFN_BLK, D), lambda b, be, nu: (blk(b, be, nu), 0)),
            wspec((D, D_EXPERT)), wspec((D, D_EXPERT)), wspec((D_EXPERT, D)),
        ],
        out_specs=pl.BlockSpec((FFN_BLK, D), lambda b, be, nu: (blk(b, be, nu), 0)),
    )
    return pl.pallas_call(
        _ffn_kernel,
        out_shape=jax.ShapeDtypeStruct(xs.shape, BF16),
        grid_spec=gs,
        input_output_aliases={2: 0},
        compiler_params=_cparams(1),
        name="moe_expert_ffn",
    )(tb["bexp"], tb["nused"], xs, w1, w3, w2)


def _combine_kernel(dst_s, tot_s,
                    slot_ref, g_ref, x32_ref, x16_ref, ys_ref, sw13_ref, sw2_ref, lng_ref, lnb_ref,
                    o32_ref, o16_ref, ybuf, acc_ref, sem, *, J, LT, alpha):
    j = pl.program_id(0)
    par = j % 2
    nt = x32_ref.shape[0]
    tot = tot_s[j]
    buf = ybuf.at[par]

    def gather_tile(step, slot):
        def copy_tile(local_row, sorted_row):
            pltpu.make_async_copy(ys_ref.at[pl.ds(sorted_row, ROW_ALIGN)],
                                  ybuf.at[slot, pl.ds(local_row, ROW_ALIGN)], sem.at[slot]).start()
        return copy_tile

    @pl.when(j == 0)
    def _():
        _start_tile_copies(gather_tile(0, 0), dst_s, 0, 0, tot_s[0])

    npair = (tot + PAIR - 1) // PAIR

    def zrows(i, c):
        zo = pl.multiple_of(tot + i * LOCAL_ALIGN, LOCAL_ALIGN)
        buf[pl.ds(zo, LOCAL_ALIGN), :] = jnp.zeros((LOCAL_ALIGN, ybuf.shape[2]), BF16)
        return c

    lax.fori_loop(0, (npair * PAIR - tot) // LOCAL_ALIGN, zrows, 0)

    xb = x16_ref[...]
    h = jnp.dot(xb, sw13_ref[...], preferred_element_type=F32)
    hh = (jax.nn.silu(h[:, :D_EXPERT]) * h[:, D_EXPERT:]).astype(BF16)
    acc_ref[...] = jnp.dot(hh, sw2_ref[...], preferred_element_type=F32)

    _wait_rows(ys_ref, buf, sem.at[par], tot)

    slots16 = slot_ref[...].astype(jnp.int16)
    gates16 = g_ref[...].astype(BF16)
    tiles_per_pair = PAIR // ROW_ALIGN
    copy_next = gather_tile(j + 1, 1 - par)

    def pair(c, carry, prefetch=False):
        if prefetch:
            for i in range(tiles_per_pair):
                t = c * tiles_per_pair + i
                copy_next(pl.multiple_of(t * ROW_ALIGN, ROW_ALIGN),
                          pl.multiple_of(dst_s[(j + 1) * LT + t], ROW_ALIGN))
        part = None
        for u in range(PAIR // SUB):
            base = pl.multiple_of(c * PAIR + u * SUB, SUB)
            siota = (lax.broadcasted_iota(I32, (SUB, nt), 0) + base).astype(jnp.int16)
            wt = jnp.zeros((SUB, nt), BF16)
            for k in range(TOP_K):
                wt = jnp.where(siota == slots16[k:k + 1, :], gates16[k:k + 1, :], wt)
            y = lax.dot_general(wt, buf[pl.ds(base, SUB), :], (((0,), (0,)), ((), ())),
                                preferred_element_type=F32)
            part = y if part is None else part + y
        acc_ref[...] += part
        return carry

    sure = SUB * TOP_K // PAIR

    @pl.when(j + 1 < J)
    def _():
        lax.fori_loop(0, sure, functools.partial(pair, prefetch=True), 0)
        _start_tile_copies(copy_next, dst_s, (j + 1) * LT, sure * tiles_per_pair, tot_s[j + 1] - sure * PAIR)

    @pl.when(j + 1 >= J)
    def _():
        lax.fori_loop(0, sure, pair, 0)

    lax.fori_loop(sure, npair, pair, 0)

    z = _layer_norm(alpha * x32_ref[...] + acc_ref[...], lng_ref[...], lnb_ref[...])
    o32_ref[...] = z
    o16_ref[...] = z.astype(BF16)


def _combine(tb, slot, gate, x32, x16, ys, sw13, sw2, ln_g, ln_b, alpha, lmax):
    T, D = x32.shape
    J = T // SUB
    full = lambda shape: pl.BlockSpec(shape, lambda j, *_: (0,) * len(shape))
    gs = pltpu.PrefetchScalarGridSpec(
        num_scalar_prefetch=2,
        grid=(J,),
        in_specs=[
            pl.BlockSpec((TOP_K, SUB), lambda j, *_: (0, j)),
            pl.BlockSpec((TOP_K, SUB), lambda j, *_: (0, j)),
            pl.BlockSpec((SUB, D), lambda j, *_: (j, 0)),
            pl.BlockSpec((SUB, D), lambda j, *_: (j, 0)),
            pl.BlockSpec(memory_space=pl.ANY),
            full(sw13.shape), full(sw2.shape), full((1, D)), full((1, D)),
        ],
        out_specs=(pl.BlockSpec((SUB, D), lambda j, *_: (j, 0)), pl.BlockSpec((SUB, D), lambda j, *_: (j, 0))),
        scratch_shapes=[
            pltpu.VMEM((2, lmax, D), BF16),
            pltpu.VMEM((SUB, D), F32),
            pltpu.SemaphoreType.DMA((2,)),
        ],
    )
    return pl.pallas_call(
        functools.partial(_combine_kernel, J=J, LT=lmax // ROW_ALIGN, alpha=alpha),
        out_shape=(jax.ShapeDtypeStruct((T, D), F32), jax.ShapeDtypeStruct((T, D), BF16)),
        grid_spec=gs,
        compiler_params=_cparams(1),
        name="moe_combine",
    )(tb["dst"].reshape(-1), tb["tot"], slot, gate, x32, x16, ys, sw13, sw2, ln_g, ln_b)


def kernel(x, positions, w_in, gate_b, w_branch, w_out, pool_w, pool_scale, rg_conv_w, rg_conv_b, rg_wa, rg_ba,
           rg_wx, rg_bx, rg_lambda, ln1_g, ln1_b, router_w, router_bias, exp_w1, exp_w3, exp_w2, sh_w1, sh_w3,
           sh_w2, ln2_g, ln2_b):
    B, S, D = x.shape
    L = w_in.shape[0]
    T = B * S
    E = router_w.shape[2]
    assert T % SUB == 0 and S % CHUNK == 0 and E == N_GROUPS * E_PER_GROUP
    alpha = (2 * L) ** 0.25
    n_a = 4 * RET_W + POOL_W + 2 * RG_W
    nblk_max = _num_blocks_max(T, E)
    lmax = _local_rows_max(E)

    cos, sin = _rope_tables(positions)
    consts = _retention_consts()
    tri = (jnp.arange(SUB)[:, None] < jnp.arange(SUB)[None, :]).astype(BF16)
    row = lambda a: a.reshape(1, -1)

    x32 = x.reshape(T, D)
    x16 = x32.astype(BF16)
    for l in range(L):
        ha = _in_proj(x16, w_in, l, 0, n_a, "in_proj_a")
        gl = _in_proj(x16, w_in, l, n_a, w_in.shape[2] - n_a, "in_proj_gates")
        br = _branches(ha, cos, sin, consts, pool_w[l].astype(BF16), row(pool_scale[l]), rg_conv_w[l],
                       row(rg_conv_b[l]), rg_wa[l].astype(BF16), row(rg_ba[l]), rg_wx[l].astype(BF16),
                       row(rg_bx[l]), row(rg_lambda[l]), B, S)
        x32, x16 = _merge(br, gl, x32, w_branch[l].astype(BF16), w_out[l].astype(BF16), gate_b[l],
                          row(ln1_g[l]), row(ln1_b[l]), alpha)

        top_e, gate, rank, cnt = _router(x32, router_w[l].T, router_bias[l].reshape(E, 1), tri)
        tb = _tables(cnt[:, :, 0], nblk_max, lmax)
        xs, slot = _dispatch(tb, top_e, rank, x16, nblk_max, lmax)
        ys = _expert_ffn(tb, xs, exp_w1, exp_w3, exp_w2, l, nblk_max)
        sw13 = jnp.concatenate([sh_w1[l], sh_w3[l]], axis=-1).astype(BF16)
        x32, x16 = _combine(tb, slot, gate, x32, x16, ys, sw13, sh_w2[l].astype(BF16),
                            row(ln2_g[l]), row(ln2_b[l]), alpha, lmax)
    return x32.reshape(B, S, D)
```

```python
import functools
import math

import jax
import jax.numpy as jnp
from jax import lax
from jax.experimental import pallas as pl
from jax.experimental.pallas import tpu as pltpu

F32, BF16, I32 = jnp.float32, jnp.bfloat16, jnp.int32

RET_HEADS, RET_DH, RET_W, CHUNK = 4, 128, 512, 128
ROPE_BASE = 10000.0
POOL_WINDOWS, POOL_GW, POOL_W = (2, 4, 8, 16), 128, 512
RG_BLOCKS, RG_BD, RG_W, RG_CONV, RG_C = 4, 128, 512, 4, 8.0
HIST = 16
N_BRANCH = 3
N_GROUPS, E_PER_GROUP, TOPK_GROUPS, TOP_K, D_EXPERT = 8, 8, 4, 8, 256
ROUTED_SCALE = 2.5
LN_EPS, GN_EPS = 1e-5, 1e-6

VMEM_LIMIT = 56 * 1024 * 1024
MM_TM, MM_TN = 2048, 512
BR_TS = 512
MG_TM = 512
SUB = 256
ROW_ALIGN = 16
ISSUE_UNROLL = 4
LOCAL_ALIGN = ROW_ALIGN * ISSUE_UNROLL
PAIR = 2 * SUB
FFN_BLK = 1024
NEG_INF = float("-inf")


def _cparams(n_axes):
    return pltpu.CompilerParams(dimension_semantics=("arbitrary",) * n_axes, vmem_limit_bytes=VMEM_LIMIT)


def _layer_norm(z, g, b):
    mu = jnp.mean(z, axis=-1, keepdims=True)
    zc = z - mu
    var = jnp.mean(zc * zc, axis=-1, keepdims=True)
    return zc * lax.rsqrt(var + LN_EPS) * g + b


def _rope_kernel(pos_ref, inv_ref, sign_ref, cos_ref, sin_ref):
    ang = pos_ref[...].astype(F32) * inv_ref[...]
    cos_ref[...] = jnp.cos(ang)
    sin_ref[...] = jnp.sin(ang) * sign_ref[...]


def _rope_tables(positions):
    S = positions.shape[0]
    half = RET_DH // 2
    inv = ROPE_BASE ** (-jnp.arange(half, dtype=F32) / half)
    inv2 = jnp.concatenate([inv, inv]).reshape(1, RET_DH)
    sign = jnp.concatenate([-jnp.ones((half,), F32), jnp.ones((half,), F32)]).reshape(1, RET_DH)
    return pl.pallas_call(
        _rope_kernel,
        out_shape=(jax.ShapeDtypeStruct((S, RET_DH), F32), jax.ShapeDtypeStruct((S, RET_DH), F32)),
        name="rope_tables",
    )(positions.reshape(S, 1), inv2, sign)


def _mm_kernel(x_ref, w_ref, o_ref):
    o_ref[...] = jnp.dot(x_ref[...], w_ref[...].astype(BF16), preferred_element_type=F32).astype(o_ref.dtype)


def _in_proj(x, w_all, layer, col0, ncols, name):
    M, K = x.shape
    tm, tn = min(MM_TM, M), MM_TN
    assert col0 % tn == 0 and ncols % tn == 0
    return pl.pallas_call(
        _mm_kernel,
        out_shape=jax.ShapeDtypeStruct((M, ncols), BF16),
        grid=(M // tm, ncols // tn),
        in_specs=[pl.BlockSpec((tm, K), lambda i, j: (i, 0)),
                  pl.BlockSpec((None, K, tn), lambda i, j: (layer, 0, j + col0 // tn))],
        out_specs=pl.BlockSpec((tm, tn), lambda i, j: (i, j)),
        compiler_params=_cparams(2),
        name=name,
    )(x, w_all)


def _log_sigmoid(x):
    return jnp.minimum(x, 0.0) - jnp.log(1.0 + jnp.exp(-jnp.abs(x)))


def _branch_kernel(ha_ref, cos_ref, sin_ref, dmask_ref, kdec_ref, qdec_ref, band_ref, shift_ref, poolw_ref,
                   pscale_ref, convw_ref, convb_ref, wa_ref, ba_ref, wx_ref, bx_ref, lam_ref,
                   o_ref, state_ref, hcar_ref, hist_ref, *, ts, cdec):
    s = pl.program_id(1)
    x0 = 4 * RET_W

    @pl.when(s == 0)
    def _():
        state_ref[...] = jnp.zeros_like(state_ref)
        hcar_ref[...] = jnp.zeros_like(hcar_ref)
        hist_ref[...] = jnp.zeros_like(hist_ref)

    log_sig_lam = _log_sigmoid(lam_ref[...])

    def chunk(c, carry):
        r0 = pl.multiple_of(c * CHUNK, CHUNK)
        rows = pl.ds(r0, CHUNK)
        cosf = cos_ref[rows, :]
        sinf = sin_ref[rows, :]

        hist_ref[CHUNK:2 * CHUNK, :] = ha_ref[rows, slice(x0, x0 + POOL_W + RG_W)]
        wins = [jnp.dot(band_ref[gi], hist_ref[:, gi * POOL_GW:(gi + 1) * POOL_GW], preferred_element_type=F32)
                for gi in range(len(POOL_WINDOWS))]
        lagged = [jnp.dot(shift_ref[kk - 1], hist_ref[:, POOL_W:POOL_W + RG_W], preferred_element_type=F32)
                  for kk in range(1, RG_CONV)]
        hist_ref[CHUNK - HIST:CHUNK, :] = hist_ref[2 * CHUNK - HIST:2 * CHUNK, :]

        def retention_head(h):
            cs = slice(h * RET_DH, (h + 1) * RET_DH)
            q = ha_ref[rows, cs].astype(F32)
            k = ha_ref[rows, slice(RET_W + h * RET_DH, RET_W + (h + 1) * RET_DH)].astype(F32)
            v = ha_ref[rows, slice(2 * RET_W + h * RET_DH, 2 * RET_W + (h + 1) * RET_DH)]
            g = ha_ref[rows, slice(3 * RET_W + h * RET_DH, 3 * RET_W + (h + 1) * RET_DH)].astype(F32)
            qr = q * cosf + pltpu.roll(q, RET_DH // 2, 1) * sinf
            kr = (k * cosf + pltpu.roll(k, RET_DH // 2, 1) * sinf) * (RET_DH ** -0.5)
            qb = qr.astype(BF16)
            sc = lax.dot_general(qb, kr.astype(BF16), (((1,), (1,)), ((), ())),
                                 preferred_element_type=F32) * dmask_ref[h]
            o = jnp.dot(sc.astype(BF16), v, preferred_element_type=F32)
            st = state_ref[h]
            o = o + jnp.dot(qb, st.astype(BF16), preferred_element_type=F32) * qdec_ref[h]
            kd = (kr * kdec_ref[h]).astype(BF16)
            state_ref[h] = st * cdec[h] + lax.dot_general(kd, v, (((0,), (0,)), ((), ())),
                                                          preferred_element_type=F32)
            mu = jnp.mean(o, axis=-1, keepdims=True)
            oc = o - mu
            var = jnp.mean(oc * oc, axis=-1, keepdims=True)
            o_ref[rows, cs] = (jax.nn.silu(g) * (oc * lax.rsqrt(var + GN_EPS))).astype(BF16)

        retention_head(0)

        xr = ha_ref[rows, slice(x0 + POOL_W, x0 + POOL_W + RG_W)].astype(F32)
        conv = convb_ref[...] + xr * convw_ref[0:1, :]
        for kk in range(1, RG_CONV):
            conv = conv + lagged[kk - 1] * convw_ref[kk:kk + 1, :]
        cb = conv.astype(BF16)
        rl = jnp.concatenate([jnp.dot(cb[:, n * RG_BD:(n + 1) * RG_BD], wa_ref[n], preferred_element_type=F32)
                              for n in range(RG_BLOCKS)], axis=1)
        il = jnp.concatenate([jnp.dot(cb[:, n * RG_BD:(n + 1) * RG_BD], wx_ref[n], preferred_element_type=F32)
                              for n in range(RG_BLOCKS)], axis=1)

        retention_head(1)

        r = jax.nn.sigmoid(rl + ba_ref[...])
        ig = jax.nn.sigmoid(il + bx_ref[...])
        log_a = RG_C * r * log_sig_lam
        a = jnp.exp(log_a)
        om = -jnp.tanh(log_a) * (a * a + 1.0)
        bb = jnp.where(om > 0.0, om * lax.rsqrt(om), 0.0) * (ig * conv)

        retention_head(2)

        row = lax.broadcasted_iota(I32, (CHUNK, RG_W), 0)
        A, Bv = a, bb
        d = 1
        while d < CHUNK:
            As = jnp.where(row < d, 1.0, pltpu.roll(A, d, 0))
            Bs = jnp.where(row < d, 0.0, pltpu.roll(Bv, d, 0))
            Bv = A * Bs + Bv
            A = A * As
            d *= 2

        retention_head(3)

        hh = A * hcar_ref[...] + Bv
        hcar_ref[...] = hh[CHUNK - 1:CHUNK, :]
        yr = ha_ref[rows, slice(x0 + POOL_W + RG_W, x0 + POOL_W + 2 * RG_W)].astype(F32)
        o_ref[rows, slice(RET_W + POOL_W, RET_W + POOL_W + RG_W)] = (hh * jax.nn.gelu(yr)).astype(BF16)

        xp = ha_ref[rows, slice(x0, x0 + POOL_W)].astype(F32)
        t_idx = s * ts + r0 + lax.broadcasted_iota(I32, (CHUNK, POOL_GW), 0)
        for gi, w in enumerate(POOL_WINDOWS):
            cs = slice(gi * POOL_GW, (gi + 1) * POOL_GW)
            n = jnp.minimum(w, t_idx + 1).astype(F32)
            pooled = wins[gi] / n - xp[:, cs]
            y = jnp.dot(pooled.astype(BF16), poolw_ref[gi], preferred_element_type=F32) * pscale_ref[:, cs]
            o_ref[rows, slice(RET_W + gi * POOL_GW, RET_W + (gi + 1) * POOL_GW)] = y.astype(BF16)
        return carry

    lax.fori_loop(0, ts // CHUNK, chunk, 0, unroll=4)


def _retention_consts():
    log_g = [math.log(1.0 - 2.0 ** (-5.0 - h)) for h in range(RET_HEADS)]
    lg = jnp.asarray(log_g, F32)
    idx = jnp.arange(CHUNK, dtype=F32)
    diff = idx[:, None] - idx[None, :]
    dmask = jnp.where(diff >= 0, jnp.exp(lg[:, None, None] * jnp.maximum(diff, 0.0)), 0.0)
    kdec = jnp.exp(lg[:, None] * (CHUNK - 1 - idx)[None, :])
    qdec = jnp.exp(lg[:, None] * (idx + 1.0)[None, :])
    kdec = jnp.broadcast_to(kdec[:, :, None], (RET_HEADS, CHUNK, RET_DH))
    qdec = jnp.broadcast_to(qdec[:, :, None], (RET_HEADS, CHUNK, RET_DH))
    cdec = tuple(math.exp(v * CHUNK) for v in log_g)
    t = jnp.arange(CHUNK)[:, None]
    c = jnp.arange(2 * CHUNK)[None, :]
    lag = CHUNK + t - c
    band = jnp.stack([(lag >= 0) & (lag < w) for w in POOL_WINDOWS]).astype(BF16)
    shift = jnp.stack([lag == k for k in range(1, RG_CONV)]).astype(BF16)
    return dmask, kdec, qdec, band, shift, cdec


def _branches(ha, cos, sin, consts, pool_w, pool_scale, conv_w, conv_b, wa, ba, wx, bx, lam, B, S):
    dmask, kdec, qdec, band, shift, cdec = consts
    ts = min(BR_TS, S)
    nst = S // ts
    HA = ha.shape[1]
    OW = RET_W + POOL_W + RG_W
    full = lambda shape: pl.BlockSpec(shape, lambda b, s: (0,) * len(shape))
    return pl.pallas_call(
        functools.partial(_branch_kernel, ts=ts, cdec=cdec),
        out_shape=jax.ShapeDtypeStruct((B * S, OW), BF16),
        grid=(B, nst),
        in_specs=[
            pl.BlockSpec((ts, HA), lambda b, s: (b * nst + s, 0)),
            pl.BlockSpec((ts, RET_DH), lambda b, s: (s, 0)),
            pl.BlockSpec((ts, RET_DH), lambda b, s: (s, 0)),
            full((RET_HEADS, CHUNK, CHUNK)), full((RET_HEADS, CHUNK, RET_DH)), full((RET_HEADS, CHUNK, RET_DH)),
            full(band.shape), full(shift.shape),
            full((len(POOL_WINDOWS), POOL_GW, POOL_GW)), full((1, POOL_W)),
            full((RG_CONV, RG_W)), full((1, RG_W)),
            full((RG_BLOCKS, RG_BD, RG_BD)), full((1, RG_W)),
            full((RG_BLOCKS, RG_BD, RG_BD)), full((1, RG_W)), full((1, RG_W)),
        ],
        out_specs=pl.BlockSpec((ts, OW), lambda b, s: (b * nst + s, 0)),
        scratch_shapes=[
            pltpu.VMEM((RET_HEADS, RET_DH, RET_DH), F32),
            pltpu.VMEM((1, RG_W), F32),
            pltpu.VMEM((2 * CHUNK, POOL_W + RG_W), BF16),
        ],
        compiler_params=_cparams(2),
        name="mixer_branches",
    )(ha, cos, sin, dmask, kdec, qdec, band, shift, pool_w, pool_scale, conv_w, conv_b, wa, ba, wx, bx, lam)


def _merge_kernel(br_ref, gl_ref, x_ref, wbr_ref, wout_ref, gb_ref, lng_ref, lnb_ref, o32_ref, o16_ref, *, alpha):
    D = x_ref.shape[1]
    acc = None
    for i in range(N_BRANCH):
        y = jnp.dot(br_ref[:, i * RET_W:(i + 1) * RET_W], wbr_ref[i], preferred_element_type=F32)
        gate = jax.nn.sigmoid(gl_ref[:, i * D:(i + 1) * D].astype(F32) + gb_ref[i:i + 1, :])
        acc = gate * y if acc is None else acc + gate * y
    m = jnp.dot(acc.astype(BF16), wout_ref[...], preferred_element_type=F32)
    z = _layer_norm(alpha * x_ref[...] + m, lng_ref[...], lnb_ref[...])
    o32_ref[...] = z
    o16_ref[...] = z.astype(BF16)


def _merge(br, gl, x32, w_branch, w_out, gate_b, ln_g, ln_b, alpha):
    T, D = x32.shape
    tm = min(MG_TM, T)
    full = lambda shape: pl.BlockSpec(shape, lambda i: (0,) * len(shape))
    return pl.pallas_call(
        functools.partial(_merge_kernel, alpha=alpha),
        out_shape=(jax.ShapeDtypeStruct((T, D), F32), jax.ShapeDtypeStruct((T, D), BF16)),
        grid=(T // tm,),
        in_specs=[
            pl.BlockSpec((tm, br.shape[1]), lambda i: (i, 0)),
            pl.BlockSpec((tm, gl.shape[1]), lambda i: (i, 0)),
            pl.BlockSpec((tm, D), lambda i: (i, 0)),
            full(w_branch.shape), full(w_out.shape), full(gate_b.shape), full((1, D)), full((1, D)),
        ],
        out_specs=(pl.BlockSpec((tm, D), lambda i: (i, 0)), pl.BlockSpec((tm, D), lambda i: (i, 0))),
        compiler_params=_cparams(1),
        name="mixer_merge",
    )(br, gl, x32, w_branch, w_out, gate_b, ln_g, ln_b)


def _first_argmax(v, iota, n):
    m = jnp.max(v, axis=0, keepdims=True)
    idx = jnp.min(jnp.where(v == m, iota, n), axis=0, keepdims=True)
    return m, idx


def _router_kernel(x_ref, xh_ref, wh_ref, wl_ref, rb_ref, tri_ref, e_ref, g_ref, rk_ref, cnt_ref):
    E = wh_ref.shape[0]
    nt = x_ref.shape[0]
    xh = xh_ref[...]
    xl = (x_ref[...] - xh.astype(F32)).astype(BF16)
    wh = wh_ref[...]
    nt_dot = lambda a, b: lax.dot_general(a, b, (((1,), (1,)), ((), ())), preferred_element_type=F32)
    logits = nt_dot(wh, xh) + (nt_dot(wh, xl) + nt_dot(wl_ref[...], xh))
    scores = jax.nn.sigmoid(logits)
    biased = scores + rb_ref[...]

    sub = lax.broadcasted_iota(I32, (E_PER_GROUP, nt), 0)
    giota = lax.broadcasted_iota(I32, (N_GROUPS, nt), 0)
    gs = jnp.zeros((N_GROUPS, nt), F32)
    for g in range(N_GROUPS):
        bg = biased[g * E_PER_GROUP:(g + 1) * E_PER_GROUP, :]
        m1, i1 = _first_argmax(bg, sub, E_PER_GROUP)
        m2 = jnp.max(jnp.where(sub == i1, NEG_INF, bg), axis=0, keepdims=True)
        gs = jnp.where(giota == g, m1 + m2, gs)
    gsel = jnp.zeros((N_GROUPS, nt), jnp.bool_)
    v = gs
    for _ in range(TOPK_GROUPS):
        _, idx = _first_argmax(v, giota, N_GROUPS)
        hit = giota == idx
        gsel = jnp.logical_or(gsel, hit)
        v = jnp.where(hit, NEG_INF, v)
    gself = jnp.where(gsel, 1.0, 0.0)
    masked = jnp.concatenate(
        [jnp.where(jnp.broadcast_to(gself[g:g + 1, :], (E_PER_GROUP, nt)) > 0.5,
                   biased[g * E_PER_GROUP:(g + 1) * E_PER_GROUP, :], NEG_INF) for g in range(N_GROUPS)], axis=0)

    eiota = lax.broadcasted_iota(I32, (E, nt), 0)
    v = masked
    onehot = jnp.zeros((E, nt), F32)
    idxs, sels = [], []
    for _ in range(TOP_K):
        _, idx = _first_argmax(v, eiota, E)
        hit = eiota == idx
        sels.append(jnp.sum(jnp.where(hit, scores, 0.0), axis=0, keepdims=True))
        idxs.append(idx)
        onehot = onehot + jnp.where(hit, 1.0, 0.0)
        v = jnp.where(hit, NEG_INF, v)
    ssum = sels[0]
    for k in range(1, TOP_K):
        ssum = ssum + sels[k]
    excl = jnp.dot(onehot.astype(BF16), tri_ref[...], preferred_element_type=F32)
    for k in range(TOP_K):
        e_ref[k:k + 1, :] = idxs[k]
        g_ref[k:k + 1, :] = sels[k] / ssum * ROUTED_SCALE
        rk_ref[k:k + 1, :] = jnp.sum(jnp.where(eiota == idxs[k], excl, 0.0), axis=0, keepdims=True).astype(I32)
    cnt_ref[...] = jnp.sum(onehot, axis=1, keepdims=True).astype(I32)


def _router(x32, x16, rwt, rb, tri):
    T, D = x32.shape
    E = rwt.shape[0]
    J = T // SUB
    wh = rwt.astype(BF16)
    wl = (rwt - wh.astype(F32)).astype(BF16)
    row = lambda dt: jax.ShapeDtypeStruct((TOP_K, T), dt)
    return pl.pallas_call(
        _router_kernel,
        out_shape=(row(I32), row(F32), row(I32), jax.ShapeDtypeStruct((J, E, 1), I32)),
        grid=(J,),
        in_specs=[
            pl.BlockSpec((SUB, D), lambda j: (j, 0)),
            pl.BlockSpec((SUB, D), lambda j: (j, 0)),
            pl.BlockSpec((E, D), lambda j: (0, 0)),
            pl.BlockSpec((E, D), lambda j: (0, 0)),
            pl.BlockSpec((E, 1), lambda j: (0, 0)),
            pl.BlockSpec((SUB, SUB), lambda j: (0, 0)),
        ],
        out_specs=(
            pl.BlockSpec((TOP_K, SUB), lambda j: (0, j)),
            pl.BlockSpec((TOP_K, SUB), lambda j: (0, j)),
            pl.BlockSpec((TOP_K, SUB), lambda j: (0, j)),
            pl.BlockSpec((None, E, 1), lambda j: (j, 0, 0)),
        ),
        compiler_params=_cparams(1),
        name="moe_router",
    )(x32, x16, wh, wl, rb, tri)


def _round_up(a, m):
    return (a + m - 1) // m * m


def _local_rows_max(E):
    return _round_up(SUB * TOP_K + E * (ROW_ALIGN - 1) + LOCAL_ALIGN - ROW_ALIGN, PAIR)


def _num_blocks_max(T, E):
    J = T // SUB
    rows = T * TOP_K + J * (E * (ROW_ALIGN - 1) + LOCAL_ALIGN - ROW_ALIGN) + E * (FFN_BLK - ROW_ALIGN)
    return -(-rows // FFN_BLK)


def _tables(cnt, nblk_max, lmax):
    J, E = cnt.shape
    pc = _round_up(cnt, ROW_ALIGN)
    tot0 = jnp.sum(pc, axis=1)
    pc = pc.at[:, E - 1].add(_round_up(tot0, LOCAL_ALIGN) - tot0)
    lo = jnp.cumsum(pc, axis=1) - pc
    tot = jnp.sum(pc, axis=1)
    se = jnp.sum(pc, axis=0)
    reg = _round_up(se, FFN_BLK)
    creg = jnp.cumsum(reg)
    gs = creg - reg
    go = gs[None, :] + jnp.cumsum(pc, axis=0) - pc
    nused = (creg[-1] // FFN_BLK).reshape(1)
    bstart = jnp.arange(nblk_max, dtype=I32) * FFN_BLK
    bexp = jnp.minimum(jnp.sum(bstart[:, None] >= creg[None, :], axis=1), E - 1)
    trow = jnp.arange(lmax // ROW_ALIGN, dtype=I32) * ROW_ALIGN
    inrun = (trow[None, :, None] >= lo[:, None, :]) & (trow[None, :, None] < (lo + pc)[:, None, :])
    dst = trow[None, :] + jnp.sum(jnp.where(inrun, (go - lo)[:, None, :], 0), axis=2)
    i32 = lambda a: a.astype(I32)
    return dict(lo=i32(lo), dst=i32(dst), tot=i32(tot), tail_start=i32(gs + se), tail_len=i32(reg - se),
                nused=i32(nused), bexp=i32(bexp))


def _wait_rows(src_ref, dst_ref, sem, nrows):
    def wait_unit(unit):
        def body(i, c):
            pltpu.make_async_copy(src_ref.at[pl.ds(0, unit)], dst_ref.at[pl.ds(0, unit)], sem).wait()
            return c
        return body

    lax.fori_loop(0, nrows // SUB, wait_unit(SUB), 0)
    lax.fori_loop(0, (nrows % SUB) // LOCAL_ALIGN, wait_unit(LOCAL_ALIGN), 0)


def _start_tile_copies(copy_tile, dst_s, table_base, first_tile, rows):
    def body(q, c):
        for u in range(ISSUE_UNROLL):
            i = first_tile + q * ISSUE_UNROLL + u
            copy_tile(pl.multiple_of(i * ROW_ALIGN, ROW_ALIGN), pl.multiple_of(dst_s[table_base + i], ROW_ALIGN))
        return c

    lax.fori_loop(0, rows // LOCAL_ALIGN, body, 0)


def _dispatch_kernel(dst_s, tot_s, tls_s, tll_s, nu_s,
                     e_ref, rk_ref, lov_ref, x_ref, xs_ref, slot_ref, xbuf, zbuf, sem, *, E, J, LT, nblk_max):
    j = pl.program_id(0)
    par = j % 2
    nt = x_ref.shape[0]

    eiota = lax.broadcasted_iota(I32, (E, nt), 0)
    for k in range(TOP_K):
        base = jnp.sum(jnp.where(eiota == e_ref[k:k + 1, :], lov_ref[...], 0), axis=0, keepdims=True)
        slot_ref[k:k + 1, :] = base + rk_ref[k:k + 1, :]
    slots16 = slot_ref[...].astype(jnp.int16)
    xb = x_ref[...]
    tot = tot_s[j]
    buf = xbuf.at[par]

    def permute_pair(c):
        for u in range(PAIR // SUB):
            base = pl.multiple_of(c * PAIR + u * SUB, SUB)
            siota = (lax.broadcasted_iota(I32, (SUB, nt), 0) + base).astype(jnp.int16)
            pb = jnp.zeros((SUB, nt), BF16)
            for k in range(TOP_K):
                pb = jnp.where(siota == slots16[k:k + 1, :], jnp.ones((), BF16), pb)
            buf[pl.ds(base, SUB), :] = jnp.dot(pb, xb, preferred_element_type=F32).astype(BF16)

    def copy_tile(local_row, sorted_row):
        pltpu.make_async_copy(buf.at[pl.ds(local_row, ROW_ALIGN)], xs_ref.at[pl.ds(sorted_row, ROW_ALIGN)],
                              sem.at[par]).start()

    tiles_per_pair = PAIR // ROW_ALIGN

    def trip(c, carry):
        for i in range(tiles_per_pair):
            t = (c - 1) * tiles_per_pair + i
            copy_tile(pl.multiple_of(t * ROW_ALIGN, ROW_ALIGN), pl.multiple_of(dst_s[j * LT + t], ROW_ALIGN))
        permute_pair(c)
        return carry

    npair = (tot + PAIR - 1) // PAIR
    permute_pair(0)
    lax.fori_loop(1, npair, trip, 0)
    _start_tile_copies(copy_tile, dst_s, j * LT, (npair - 1) * tiles_per_pair, tot - (npair - 1) * PAIR)

    ztile = zbuf.at[pl.ds(0, ROW_ALIGN)]

    @pl.when(j == 0)
    def _():
        zbuf[...] = jnp.zeros_like(zbuf)

        def per_expert_tail(e, carry):
            def per_tile(i, c):
                do = pl.multiple_of(tls_s[e] + i * ROW_ALIGN, ROW_ALIGN)
                pltpu.make_async_copy(ztile, xs_ref.at[pl.ds(do, ROW_ALIGN)], sem.at[2]).start()
                return c

            lax.fori_loop(0, tll_s[e] // ROW_ALIGN, per_tile, 0)
            return carry

        lax.fori_loop(0, E, per_expert_tail, 0)

        def per_block(b, c):
            do = pl.multiple_of(b * FFN_BLK, FFN_BLK)
            pltpu.make_async_copy(zbuf, xs_ref.at[pl.ds(do, FFN_BLK)], sem.at[2]).start()
            return c

        lax.fori_loop(nu_s[0], nblk_max, per_block, 0)

    @pl.when(j > 0)
    def _():
        _wait_rows(xbuf.at[1 - par], xs_ref, sem.at[1 - par], tot_s[j - 1])

    @pl.when(j == J - 1)
    def _():
        _wait_rows(buf, xs_ref, sem.at[par], tot)

        def wtail(i, c):
            pltpu.make_async_copy(ztile, xs_ref.at[pl.ds(0, ROW_ALIGN)], sem.at[2]).wait()
            return c

        def wblock(b, c):
            pltpu.make_async_copy(zbuf, xs_ref.at[pl.ds(0, FFN_BLK)], sem.at[2]).wait()
            return c

        ntile = lax.fori_loop(0, E, lambda e, n: n + tll_s[e] // ROW_ALIGN, 0)
        lax.fori_loop(0, ntile, wtail, 0)
        lax.fori_loop(nu_s[0], nblk_max, wblock, 0)


def _dispatch(tb, top_e, rank, x16, nblk_max, lmax):
    T, D = x16.shape
    J, E = tb["lo"].shape
    gs = pltpu.PrefetchScalarGridSpec(
        num_scalar_prefetch=5,
        grid=(J,),
        in_specs=[
            pl.BlockSpec((TOP_K, SUB), lambda j, *_: (0, j)),
            pl.BlockSpec((TOP_K, SUB), lambda j, *_: (0, j)),
            pl.BlockSpec((None, E, 1), lambda j, *_: (j, 0, 0)),
            pl.BlockSpec((SUB, D), lambda j, *_: (j, 0)),
        ],
        out_specs=(pl.BlockSpec(memory_space=pl.ANY), pl.BlockSpec((TOP_K, SUB), lambda j, *_: (0, j))),
        scratch_shapes=[
            pltpu.VMEM((2, lmax, D), BF16),
            pltpu.VMEM((FFN_BLK, D), BF16),
            pltpu.SemaphoreType.DMA((3,)),
        ],
    )
    return pl.pallas_call(
        functools.partial(_dispatch_kernel, E=E, J=J, LT=lmax // ROW_ALIGN, nblk_max=nblk_max),
        out_shape=(jax.ShapeDtypeStruct((nblk_max * FFN_BLK, D), BF16), jax.ShapeDtypeStruct((TOP_K, T), I32)),
        grid_spec=gs,
        compiler_params=_cparams(1),
        name="moe_dispatch",
    )(tb["dst"].reshape(-1), tb["tot"], tb["tail_start"], tb["tail_len"], tb["nused"],
      top_e, rank, tb["lo"].reshape(J, E, 1), x16)


def _ffn_kernel(be_s, nu_s, x_ref, w1_ref, w3_ref, w2_ref, y_ref):
    @pl.when(pl.program_id(0) < nu_s[0])
    def _():
        xb = x_ref[...]
        h1 = jnp.dot(xb, w1_ref[...].astype(BF16), preferred_element_type=F32)
        h3 = jnp.dot(xb, w3_ref[...].astype(BF16), preferred_element_type=F32)
        hh = (jax.nn.silu(h1) * h3).astype(BF16)
        y_ref[...] = jnp.dot(hh, w2_ref[...].astype(BF16), preferred_element_type=F32).astype(BF16)


def _expert_ffn(tb, xs, w1, w3, w2, layer, nblk_max):
    D = xs.shape[1]
    blk = lambda b, be, nu: jnp.minimum(b, nu[0] - 1)
    wspec = lambda shape: pl.BlockSpec((None, None) + shape, lambda b, be, nu: (layer, be[blk(b, be, nu)], 0, 0))
    gs = pltpu.PrefetchScalarGridSpec(
        num_scalar_prefetch=2,
        grid=(nblk_max,),
        in_specs=[
            pl.BlockSpec((FFN_BLK, D), lambda b, be, nu: (blk(b, be, nu), 0)),
            wspec((D, D_EXPERT)), wspec((D, D_EXPERT)), wspec((D_EXPERT, D)),
        ],
        out_specs=pl.BlockSpec((FFN_BLK, D), lambda b, be, nu: (blk(b, be, nu), 0)),
    )
    return pl.pallas_call(
        _ffn_kernel,
        out_shape=jax.ShapeDtypeStruct(xs.shape, BF16),
        grid_spec=gs,
        input_output_aliases={2: 0},
        compiler_params=_cparams(1),
        name="moe_expert_ffn",
    )(tb["bexp"], tb["nused"], xs, w1, w3, w2)


def _combine_kernel(dst_s, tot_s,
                    slot_ref, g_ref, x32_ref, x16_ref, ys_ref, sw13_ref, sw2_ref, lng_ref, lnb_ref,
                    o32_ref, o16_ref, ybuf, acc_ref, sem, *, J, LT, alpha):
    j = pl.program_id(0)
    par = j % 2
    nt = x32_ref.shape[0]
    tot = tot_s[j]
    buf = ybuf.at[par]

    def gather_tile(step, slot):
        def copy_tile(local_row, sorted_row):
            pltpu.make_async_copy(ys_ref.at[pl.ds(sorted_row, ROW_ALIGN)],
                                  ybuf.at[slot, pl.ds(local_row, ROW_ALIGN)], sem.at[slot]).start()
        return copy_tile

    @pl.when(j == 0)
    def _():
        _start_tile_copies(gather_tile(0, 0), dst_s, 0, 0, tot_s[0])

    npair = (tot + PAIR - 1) // PAIR

    def zrows(i, c):
        zo = pl.multiple_of(tot + i * LOCAL_ALIGN, LOCAL_ALIGN)
        buf[pl.ds(zo, LOCAL_ALIGN), :] = jnp.zeros((LOCAL_ALIGN, ybuf.shape[2]), BF16)
        return c

    lax.fori_loop(0, (npair * PAIR - tot) // LOCAL_ALIGN, zrows, 0)

    xb = x16_ref[...]
    h = jnp.dot(xb, sw13_ref[...], preferred_element_type=F32)
    hh = (jax.nn.silu(h[:, :D_EXPERT]) * h[:, D_EXPERT:]).astype(BF16)
    acc_ref[...] = jnp.dot(hh, sw2_ref[...], preferred_element_type=F32)

    _wait_rows(ys_ref, buf, sem.at[par], tot)

    slots16 = slot_ref[...].astype(jnp.int16)
    gates16 = g_ref[...].astype(BF16)
    tiles_per_pair = PAIR // ROW_ALIGN
    copy_next = gather_tile(j + 1, 1 - par)

    def pair(c, carry, prefetch=False):
        if prefetch:
            for i in range(tiles_per_pair):
                t = c * tiles_per_pair + i
                copy_next(pl.multiple_of(t * ROW_ALIGN, ROW_ALIGN),
                          pl.multiple_of(dst_s[(j + 1) * LT + t], ROW_ALIGN))
        part = None
        for u in range(PAIR // SUB):
            base = pl.multiple_of(c * PAIR + u * SUB, SUB)
            siota = (lax.broadcasted_iota(I32, (SUB, nt), 0) + base).astype(jnp.int16)
            wt = jnp.zeros((SUB, nt), BF16)
            for k in range(TOP_K):
                wt = jnp.where(siota == slots16[k:k + 1, :], gates16[k:k + 1, :], wt)
            y = lax.dot_general(wt, buf[pl.ds(base, SUB), :], (((0,), (0,)), ((), ())),
                                preferred_element_type=F32)
            part = y if part is None else part + y
        acc_ref[...] += part
        return carry

    sure = SUB * TOP_K // PAIR

    @pl.when(j + 1 < J)
    def _():
        lax.fori_loop(0, sure, functools.partial(pair, prefetch=True), 0)
        _start_tile_copies(copy_next, dst_s, (j + 1) * LT, sure * tiles_per_pair, tot_s[j + 1] - sure * PAIR)

    @pl.when(j + 1 >= J)
    def _():
        lax.fori_loop(0, sure, pair, 0)

    lax.fori_loop(sure, npair, pair, 0)

    z = _layer_norm(alpha * x32_ref[...] + acc_ref[...], lng_ref[...], lnb_ref[...])
    o32_ref[...] = z
    o16_ref[...] = z.astype(BF16)


def _combine(tb, slot, gate, x32, x16, ys, sw13, sw2, ln_g, ln_b, alpha, lmax):
    T, D = x32.shape
    J = T // SUB
    full = lambda shape: pl.BlockSpec(shape, lambda j, *_: (0,) * len(shape))
    gs = pltpu.PrefetchScalarGridSpec(
        num_scalar_prefetch=2,
        grid=(J,),
        in_specs=[
            pl.BlockSpec((TOP_K, SUB), lambda j, *_: (0, j)),
            pl.BlockSpec((TOP_K, SUB), lambda j, *_: (0, j)),
            pl.BlockSpec((SUB, D), lambda j, *_: (j, 0)),
            pl.BlockSpec((SUB, D), lambda j, *_: (j, 0)),
            pl.BlockSpec(memory_space=pl.ANY),
            full(sw13.shape), full(sw2.shape), full((1, D)), full((1, D)),
        ],
        out_specs=(pl.BlockSpec((SUB, D), lambda j, *_: (j, 0)), pl.BlockSpec((SUB, D), lambda j, *_: (j, 0))),
        scratch_shapes=[
            pltpu.VMEM((2, lmax, D), BF16),
            pltpu.VMEM((SUB, D), F32),
            pltpu.SemaphoreType.DMA((2,)),
        ],
    )
    return pl.pallas_call(
        functools.partial(_combine_kernel, J=J, LT=lmax // ROW_ALIGN, alpha=alpha),
        out_shape=(jax.ShapeDtypeStruct((T, D), F32), jax.ShapeDtypeStruct((T, D), BF16)),
        grid_spec=gs,
        compiler_params=_cparams(1),
        name="moe_combine",
    )(tb["dst"].reshape(-1), tb["tot"], slot, gate, x32, x16, ys, sw13, sw2, ln_g, ln_b)


def kernel(x, positions, w_in, gate_b, w_branch, w_out, pool_w, pool_scale, rg_conv_w, rg_conv_b, rg_wa, rg_ba,
           rg_wx, rg_bx, rg_lambda, ln1_g, ln1_b, router_w, router_bias, exp_w1, exp_w3, exp_w2, sh_w1, sh_w3,
           sh_w2, ln2_g, ln2_b):
    B, S, D = x.shape
    L = w_in.shape[0]
    T = B * S
    E = router_w.shape[2]
    assert T % SUB == 0 and S % CHUNK == 0 and E == N_GROUPS * E_PER_GROUP
    alpha = (2 * L) ** 0.25
    n_a = 4 * RET_W + POOL_W + 2 * RG_W
    nblk_max = _num_blocks_max(T, E)
    lmax = _local_rows_max(E)

    cos, sin = _rope_tables(positions)
    consts = _retention_consts()
    tri = (jnp.arange(SUB)[:, None] < jnp.arange(SUB)[None, :]).astype(BF16)
    row = lambda a: a.reshape(1, -1)

    x32 = x.reshape(T, D)
    x16 = x32.astype(BF16)
    for l in range(L):
        ha = _in_proj(x16, w_in, l, 0, n_a, "in_proj_a")
        gl = _in_proj(x16, w_in, l, n_a, w_in.shape[2] - n_a, "in_proj_gates")
        br = _branches(ha, cos, sin, consts, pool_w[l].astype(BF16), row(pool_scale[l]), rg_conv_w[l],
                       row(rg_conv_b[l]), rg_wa[l].astype(BF16), row(rg_ba[l]), rg_wx[l].astype(BF16),
                       row(rg_bx[l]), row(rg_lambda[l]), B, S)
        x32, x16 = _merge(br, gl, x32, w_branch[l].astype(BF16), w_out[l].astype(BF16), gate_b[l],
                          row(ln1_g[l]), row(ln1_b[l]), alpha)

        top_e, gate, rank, cnt = _router(x32, x16, router_w[l].T, router_bias[l].reshape(E, 1), tri)
        tb = _tables(cnt[:, :, 0], nblk_max, lmax)
        xs, slot = _dispatch(tb, top_e, rank, x16, nblk_max, lmax)
        ys = _expert_ffn(tb, xs, exp_w1, exp_w3, exp_w2, l, nblk_max)
        sw13 = jnp.concatenate([sh_w1[l], sh_w3[l]], axis=-1).astype(BF16)
        x32, x16 = _combine(tb, slot, gate, x32, x16, ys, sw13, sh_w2[l].astype(BF16),
                            row(ln2_g[l]), row(ln2_b[l]), alpha, lmax)
    return x32.reshape(B, S, D)
```

```python
import functools
import math

import jax
import jax.numpy as jnp
from jax import lax
from jax.experimental import pallas as pl
from jax.experimental.pallas import tpu as pltpu

F32, BF16, I32 = jnp.float32, jnp.bfloat16, jnp.int32

RET_HEADS, RET_DH, RET_W, CHUNK = 4, 128, 512, 128
ROPE_BASE = 10000.0
POOL_WINDOWS, POOL_GW, POOL_W = (2, 4, 8, 16), 128, 512
RG_BLOCKS, RG_BD, RG_W, RG_CONV, RG_C = 4, 128, 512, 4, 8.0
HIST = 16
N_BRANCH = 3
N_GROUPS, E_PER_GROUP, TOPK_GROUPS, TOP_K, D_EXPERT = 8, 8, 4, 8, 256
ROUTED_SCALE = 2.5
LN_EPS, GN_EPS = 1e-5, 1e-6

VMEM_LIMIT = 56 * 1024 * 1024
MM_TM, MM_TN = 2048, 512
BR_TS = 512
MG_TM = 512
SUB = 256
ROW_ALIGN = 16
ISSUE_UNROLL = 4
DMA_QUEUES = 2
LOCAL_ALIGN = ROW_ALIGN * ISSUE_UNROLL
PAIR = 2 * SUB
FFN_BLK = 1024
NEG_INF = float("-inf")


def _cparams(n_axes):
    return pltpu.CompilerParams(dimension_semantics=("arbitrary",) * n_axes, vmem_limit_bytes=VMEM_LIMIT)


def _layer_norm(z, g, b):
    mu = jnp.mean(z, axis=-1, keepdims=True)
    zc = z - mu
    var = jnp.mean(zc * zc, axis=-1, keepdims=True)
    return zc * lax.rsqrt(var + LN_EPS) * g + b


def _rope_kernel(pos_ref, inv_ref, sign_ref, cos_ref, sin_ref):
    ang = pos_ref[...].astype(F32) * inv_ref[...]
    cos_ref[...] = jnp.cos(ang)
    sin_ref[...] = jnp.sin(ang) * sign_ref[...]


def _rope_tables(positions):
    S = positions.shape[0]
    half = RET_DH // 2
    inv = ROPE_BASE ** (-jnp.arange(half, dtype=F32) / half)
    inv2 = jnp.concatenate([inv, inv]).reshape(1, RET_DH)
    sign = jnp.concatenate([-jnp.ones((half,), F32), jnp.ones((half,), F32)]).reshape(1, RET_DH)
    return pl.pallas_call(
        _rope_kernel,
        out_shape=(jax.ShapeDtypeStruct((S, RET_DH), F32), jax.ShapeDtypeStruct((S, RET_DH), F32)),
        name="rope_tables",
    )(positions.reshape(S, 1), inv2, sign)


def _mm_kernel(x_ref, w_ref, o_ref):
    o_ref[...] = jnp.dot(x_ref[...], w_ref[...].astype(BF16), preferred_element_type=F32).astype(o_ref.dtype)


def _in_proj(x, w_all, layer, col0, ncols, name):
    M, K = x.shape
    tm, tn = min(MM_TM, M), MM_TN
    assert col0 % tn == 0 and ncols % tn == 0
    return pl.pallas_call(
        _mm_kernel,
        out_shape=jax.ShapeDtypeStruct((M, ncols), BF16),
        grid=(M // tm, ncols // tn),
        in_specs=[pl.BlockSpec((tm, K), lambda i, j: (i, 0)),
                  pl.BlockSpec((None, K, tn), lambda i, j: (layer, 0, j + col0 // tn))],
        out_specs=pl.BlockSpec((tm, tn), lambda i, j: (i, j)),
        compiler_params=_cparams(2),
        name=name,
    )(x, w_all)


def _log_sigmoid(x):
    return jnp.minimum(x, 0.0) - jnp.log(1.0 + jnp.exp(-jnp.abs(x)))


def _branch_kernel(ha_ref, cos_ref, sin_ref, dmask_ref, kdec_ref, qdec_ref, band_ref, shift_ref, poolw_ref,
                   pscale_ref, convw_ref, convb_ref, wa_ref, ba_ref, wx_ref, bx_ref, lam_ref,
                   o_ref, state_ref, hcar_ref, hist_ref, *, ts, cdec):
    s = pl.program_id(1)
    x0 = 4 * RET_W

    @pl.when(s == 0)
    def _():
        state_ref[...] = jnp.zeros_like(state_ref)
        hcar_ref[...] = jnp.zeros_like(hcar_ref)
        hist_ref[...] = jnp.zeros_like(hist_ref)

    log_sig_lam = _log_sigmoid(lam_ref[...])

    def chunk(c, carry):
        r0 = pl.multiple_of(c * CHUNK, CHUNK)
        rows = pl.ds(r0, CHUNK)
        cosf = cos_ref[rows, :]
        sinf = sin_ref[rows, :]

        hist_ref[CHUNK:2 * CHUNK, :] = ha_ref[rows, slice(x0, x0 + POOL_W + RG_W)]
        wins = [jnp.dot(band_ref[gi], hist_ref[:, gi * POOL_GW:(gi + 1) * POOL_GW], preferred_element_type=F32)
                for gi in range(len(POOL_WINDOWS))]
        lagged = [jnp.dot(shift_ref[kk - 1], hist_ref[:, POOL_W:POOL_W + RG_W], preferred_element_type=F32)
                  for kk in range(1, RG_CONV)]
        hist_ref[CHUNK - HIST:CHUNK, :] = hist_ref[2 * CHUNK - HIST:2 * CHUNK, :]

        def retention_head(h):
            cs = slice(h * RET_DH, (h + 1) * RET_DH)
            q = ha_ref[rows, cs].astype(F32)
            k = ha_ref[rows, slice(RET_W + h * RET_DH, RET_W + (h + 1) * RET_DH)].astype(F32)
            v = ha_ref[rows, slice(2 * RET_W + h * RET_DH, 2 * RET_W + (h + 1) * RET_DH)]
            g = ha_ref[rows, slice(3 * RET_W + h * RET_DH, 3 * RET_W + (h + 1) * RET_DH)].astype(F32)
            qr = q * cosf + pltpu.roll(q, RET_DH // 2, 1) * sinf
            kr = (k * cosf + pltpu.roll(k, RET_DH // 2, 1) * sinf) * (RET_DH ** -0.5)
            qb = qr.astype(BF16)
            sc = lax.dot_general(qb, kr.astype(BF16), (((1,), (1,)), ((), ())),
                                 preferred_element_type=F32) * dmask_ref[h]
            o = jnp.dot(sc.astype(BF16), v, preferred_element_type=F32)
            st = state_ref[h]
            o = o + jnp.dot(qb, st.astype(BF16), preferred_element_type=F32) * qdec_ref[h]
            kd = (kr * kdec_ref[h]).astype(BF16)
            state_ref[h] = st * cdec[h] + lax.dot_general(kd, v, (((0,), (0,)), ((), ())),
                                                          preferred_element_type=F32)
            mu = jnp.mean(o, axis=-1, keepdims=True)
            oc = o - mu
            var = jnp.mean(oc * oc, axis=-1, keepdims=True)
            o_ref[rows, cs] = (jax.nn.silu(g) * (oc * lax.rsqrt(var + GN_EPS))).astype(BF16)

        retention_head(0)

        xr = ha_ref[rows, slice(x0 + POOL_W, x0 + POOL_W + RG_W)].astype(F32)
        conv = convb_ref[...] + xr * convw_ref[0:1, :]
        for kk in range(1, RG_CONV):
            conv = conv + lagged[kk - 1] * convw_ref[kk:kk + 1, :]
        cb = conv.astype(BF16)
        rl = jnp.concatenate([jnp.dot(cb[:, n * RG_BD:(n + 1) * RG_BD], wa_ref[n], preferred_element_type=F32)
                              for n in range(RG_BLOCKS)], axis=1)
        il = jnp.concatenate([jnp.dot(cb[:, n * RG_BD:(n + 1) * RG_BD], wx_ref[n], preferred_element_type=F32)
                              for n in range(RG_BLOCKS)], axis=1)

        retention_head(1)

        r = jax.nn.sigmoid(rl + ba_ref[...])
        ig = jax.nn.sigmoid(il + bx_ref[...])
        log_a = RG_C * r * log_sig_lam
        a = jnp.exp(log_a)
        om = -jnp.tanh(log_a) * (a * a + 1.0)
        bb = jnp.where(om > 0.0, om * lax.rsqrt(om), 0.0) * (ig * conv)

        retention_head(2)

        row = lax.broadcasted_iota(I32, (CHUNK, RG_W), 0)
        A, Bv = a, bb
        d = 1
        while d < CHUNK:
            As = jnp.where(row < d, 1.0, pltpu.roll(A, d, 0))
            Bs = jnp.where(row < d, 0.0, pltpu.roll(Bv, d, 0))
            Bv = A * Bs + Bv
            A = A * As
            d *= 2

        retention_head(3)

        hh = A * hcar_ref[...] + Bv
        hcar_ref[...] = hh[CHUNK - 1:CHUNK, :]
        yr = ha_ref[rows, slice(x0 + POOL_W + RG_W, x0 + POOL_W + 2 * RG_W)].astype(F32)
        o_ref[rows, slice(RET_W + POOL_W, RET_W + POOL_W + RG_W)] = (hh * jax.nn.gelu(yr)).astype(BF16)

        xp = ha_ref[rows, slice(x0, x0 + POOL_W)].astype(F32)
        t_idx = s * ts + r0 + lax.broadcasted_iota(I32, (CHUNK, POOL_GW), 0)
        for gi, w in enumerate(POOL_WINDOWS):
            cs = slice(gi * POOL_GW, (gi + 1) * POOL_GW)
            n = jnp.minimum(w, t_idx + 1).astype(F32)
            pooled = wins[gi] / n - xp[:, cs]
            y = jnp.dot(pooled.astype(BF16), poolw_ref[gi], preferred_element_type=F32) * pscale_ref[:, cs]
            o_ref[rows, slice(RET_W + gi * POOL_GW, RET_W + (gi + 1) * POOL_GW)] = y.astype(BF16)
        return carry

    lax.fori_loop(0, ts // CHUNK, chunk, 0, unroll=4)


def _retention_consts():
    log_g = [math.log(1.0 - 2.0 ** (-5.0 - h)) for h in range(RET_HEADS)]
    lg = jnp.asarray(log_g, F32)
    idx = jnp.arange(CHUNK, dtype=F32)
    diff = idx[:, None] - idx[None, :]
    dmask = jnp.where(diff >= 0, jnp.exp(lg[:, None, None] * jnp.maximum(diff, 0.0)), 0.0)
    kdec = jnp.exp(lg[:, None] * (CHUNK - 1 - idx)[None, :])
    qdec = jnp.exp(lg[:, None] * (idx + 1.0)[None, :])
    kdec = jnp.broadcast_to(kdec[:, :, None], (RET_HEADS, CHUNK, RET_DH))
    qdec = jnp.broadcast_to(qdec[:, :, None], (RET_HEADS, CHUNK, RET_DH))
    cdec = tuple(math.exp(v * CHUNK) for v in log_g)
    t = jnp.arange(CHUNK)[:, None]
    c = jnp.arange(2 * CHUNK)[None, :]
    lag = CHUNK + t - c
    band = jnp.stack([(lag >= 0) & (lag < w) for w in POOL_WINDOWS]).astype(BF16)
    shift = jnp.stack([lag == k for k in range(1, RG_CONV)]).astype(BF16)
    return dmask, kdec, qdec, band, shift, cdec


def _branches(ha, cos, sin, consts, pool_w, pool_scale, conv_w, conv_b, wa, ba, wx, bx, lam, B, S):
    dmask, kdec, qdec, band, shift, cdec = consts
    ts = min(BR_TS, S)
    nst = S // ts
    HA = ha.shape[1]
    OW = RET_W + POOL_W + RG_W
    full = lambda shape: pl.BlockSpec(shape, lambda b, s: (0,) * len(shape))
    return pl.pallas_call(
        functools.partial(_branch_kernel, ts=ts, cdec=cdec),
        out_shape=jax.ShapeDtypeStruct((B * S, OW), BF16),
        grid=(B, nst),
        in_specs=[
            pl.BlockSpec((ts, HA), lambda b, s: (b * nst + s, 0)),
            pl.BlockSpec((ts, RET_DH), lambda b, s: (s, 0)),
            pl.BlockSpec((ts, RET_DH), lambda b, s: (s, 0)),
            full((RET_HEADS, CHUNK, CHUNK)), full((RET_HEADS, CHUNK, RET_DH)), full((RET_HEADS, CHUNK, RET_DH)),
            full(band.shape), full(shift.shape),
            full((len(POOL_WINDOWS), POOL_GW, POOL_GW)), full((1, POOL_W)),
            full((RG_CONV, RG_W)), full((1, RG_W)),
            full((RG_BLOCKS, RG_BD, RG_BD)), full((1, RG_W)),
            full((RG_BLOCKS, RG_BD, RG_BD)), full((1, RG_W)), full((1, RG_W)),
        ],
        out_specs=pl.BlockSpec((ts, OW), lambda b, s: (b * nst + s, 0)),
        scratch_shapes=[
            pltpu.VMEM((RET_HEADS, RET_DH, RET_DH), F32),
            pltpu.VMEM((1, RG_W), F32),
            pltpu.VMEM((2 * CHUNK, POOL_W + RG_W), BF16),
        ],
        compiler_params=_cparams(2),
        name="mixer_branches",
    )(ha, cos, sin, dmask, kdec, qdec, band, shift, pool_w, pool_scale, conv_w, conv_b, wa, ba, wx, bx, lam)


def _merge_kernel(br_ref, gl_ref, x_ref, wbr_ref, wout_ref, gb_ref, lng_ref, lnb_ref, o32_ref, o16_ref, *, alpha):
    D = x_ref.shape[1]
    acc = None
    for i in range(N_BRANCH):
        y = jnp.dot(br_ref[:, i * RET_W:(i + 1) * RET_W], wbr_ref[i], preferred_element_type=F32)
        gate = jax.nn.sigmoid(gl_ref[:, i * D:(i + 1) * D].astype(F32) + gb_ref[i:i + 1, :])
        acc = gate * y if acc is None else acc + gate * y
    m = jnp.dot(acc.astype(BF16), wout_ref[...], preferred_element_type=F32)
    z = _layer_norm(alpha * x_ref[...] + m, lng_ref[...], lnb_ref[...])
    o32_ref[...] = z
    o16_ref[...] = z.astype(BF16)


def _merge(br, gl, x32, w_branch, w_out, gate_b, ln_g, ln_b, alpha):
    T, D = x32.shape
    tm = min(MG_TM, T)
    full = lambda shape: pl.BlockSpec(shape, lambda i: (0,) * len(shape))
    return pl.pallas_call(
        functools.partial(_merge_kernel, alpha=alpha),
        out_shape=(jax.ShapeDtypeStruct((T, D), F32), jax.ShapeDtypeStruct((T, D), BF16)),
        grid=(T // tm,),
        in_specs=[
            pl.BlockSpec((tm, br.shape[1]), lambda i: (i, 0)),
            pl.BlockSpec((tm, gl.shape[1]), lambda i: (i, 0)),
            pl.BlockSpec((tm, D), lambda i: (i, 0)),
            full(w_branch.shape), full(w_out.shape), full(gate_b.shape), full((1, D)), full((1, D)),
        ],
        out_specs=(pl.BlockSpec((tm, D), lambda i: (i, 0)), pl.BlockSpec((tm, D), lambda i: (i, 0))),
        compiler_params=_cparams(1),
        name="mixer_merge",
    )(br, gl, x32, w_branch, w_out, gate_b, ln_g, ln_b)


def _first_argmax(v, iota, n):
    m = jnp.max(v, axis=0, keepdims=True)
    idx = jnp.min(jnp.where(v == m, iota, n), axis=0, keepdims=True)
    return m, idx


def _router_kernel(x_ref, xh_ref, wh_ref, wl_ref, rb_ref, tri_ref, e_ref, g_ref, rk_ref, cnt_ref):
    E = wh_ref.shape[0]
    nt = x_ref.shape[0]
    xh = xh_ref[...]
    xl = (x_ref[...] - xh.astype(F32)).astype(BF16)
    wh = wh_ref[...]
    nt_dot = lambda a, b: lax.dot_general(a, b, (((1,), (1,)), ((), ())), preferred_element_type=F32)
    logits = nt_dot(wh, xh) + (nt_dot(wh, xl) + nt_dot(wl_ref[...], xh))
    scores = jax.nn.sigmoid(logits)
    biased = scores + rb_ref[...]

    sub = lax.broadcasted_iota(I32, (E_PER_GROUP, nt), 0)
    giota = lax.broadcasted_iota(I32, (N_GROUPS, nt), 0)
    gs = jnp.zeros((N_GROUPS, nt), F32)
    for g in range(N_GROUPS):
        bg = biased[g * E_PER_GROUP:(g + 1) * E_PER_GROUP, :]
        m1, i1 = _first_argmax(bg, sub, E_PER_GROUP)
        m2 = jnp.max(jnp.where(sub == i1, NEG_INF, bg), axis=0, keepdims=True)
        gs = jnp.where(giota == g, m1 + m2, gs)
    gsel = jnp.zeros((N_GROUPS, nt), jnp.bool_)
    v = gs
    for _ in range(TOPK_GROUPS):
        _, idx = _first_argmax(v, giota, N_GROUPS)
        hit = giota == idx
        gsel = jnp.logical_or(gsel, hit)
        v = jnp.where(hit, NEG_INF, v)
    gself = jnp.where(gsel, 1.0, 0.0)
    masked = jnp.concatenate(
        [jnp.where(jnp.broadcast_to(gself[g:g + 1, :], (E_PER_GROUP, nt)) > 0.5,
                   biased[g * E_PER_GROUP:(g + 1) * E_PER_GROUP, :], NEG_INF) for g in range(N_GROUPS)], axis=0)

    eiota = lax.broadcasted_iota(I32, (E, nt), 0)
    v = masked
    onehot = jnp.zeros((E, nt), F32)
    idxs, sels = [], []
    for _ in range(TOP_K):
        _, idx = _first_argmax(v, eiota, E)
        hit = eiota == idx
        sels.append(jnp.sum(jnp.where(hit, scores, 0.0), axis=0, keepdims=True))
        idxs.append(idx)
        onehot = onehot + jnp.where(hit, 1.0, 0.0)
        v = jnp.where(hit, NEG_INF, v)
    ssum = sels[0]
    for k in range(1, TOP_K):
        ssum = ssum + sels[k]
    excl = jnp.dot(onehot.astype(BF16), tri_ref[...], preferred_element_type=F32)
    for k in range(TOP_K):
        e_ref[k:k + 1, :] = idxs[k]
        g_ref[k:k + 1, :] = sels[k] / ssum * ROUTED_SCALE
        rk_ref[k:k + 1, :] = jnp.sum(jnp.where(eiota == idxs[k], excl, 0.0), axis=0, keepdims=True).astype(I32)
    cnt_ref[...] = jnp.sum(onehot, axis=1, keepdims=True).astype(I32)


def _router(x32, x16, rwt, rb, tri):
    T, D = x32.shape
    E = rwt.shape[0]
    J = T // SUB
    wh = rwt.astype(BF16)
    wl = (rwt - wh.astype(F32)).astype(BF16)
    row = lambda dt: jax.ShapeDtypeStruct((TOP_K, T), dt)
    return pl.pallas_call(
        _router_kernel,
        out_shape=(row(I32), row(F32), row(I32), jax.ShapeDtypeStruct((J, E, 1), I32)),
        grid=(J,),
        in_specs=[
            pl.BlockSpec((SUB, D), lambda j: (j, 0)),
            pl.BlockSpec((SUB, D), lambda j: (j, 0)),
            pl.BlockSpec((E, D), lambda j: (0, 0)),
            pl.BlockSpec((E, D), lambda j: (0, 0)),
            pl.BlockSpec((E, 1), lambda j: (0, 0)),
            pl.BlockSpec((SUB, SUB), lambda j: (0, 0)),
        ],
        out_specs=(
            pl.BlockSpec((TOP_K, SUB), lambda j: (0, j)),
            pl.BlockSpec((TOP_K, SUB), lambda j: (0, j)),
            pl.BlockSpec((TOP_K, SUB), lambda j: (0, j)),
            pl.BlockSpec((None, E, 1), lambda j: (j, 0, 0)),
        ),
        compiler_params=_cparams(1),
        name="moe_router",
    )(x32, x16, wh, wl, rb, tri)


def _round_up(a, m):
    return (a + m - 1) // m * m


def _local_rows_max(E):
    return _round_up(SUB * TOP_K + E * (ROW_ALIGN - 1) + LOCAL_ALIGN - ROW_ALIGN, PAIR)


def _num_blocks_max(T, E):
    J = T // SUB
    rows = T * TOP_K + J * (E * (ROW_ALIGN - 1) + LOCAL_ALIGN - ROW_ALIGN) + E * (FFN_BLK - ROW_ALIGN)
    return -(-rows // FFN_BLK)


def _tables(cnt, nblk_max, lmax):
    J, E = cnt.shape
    pc = _round_up(cnt, ROW_ALIGN)
    tot0 = jnp.sum(pc, axis=1)
    pc = pc.at[:, E - 1].add(_round_up(tot0, LOCAL_ALIGN) - tot0)
    lo = jnp.cumsum(pc, axis=1) - pc
    tot = jnp.sum(pc, axis=1)
    se = jnp.sum(pc, axis=0)
    reg = _round_up(se, FFN_BLK)
    creg = jnp.cumsum(reg)
    gs = creg - reg
    go = gs[None, :] + jnp.cumsum(pc, axis=0) - pc
    nused = (creg[-1] // FFN_BLK).reshape(1)
    bstart = jnp.arange(nblk_max, dtype=I32) * FFN_BLK
    bexp = jnp.minimum(jnp.sum(bstart[:, None] >= creg[None, :], axis=1), E - 1)
    trow = jnp.arange(lmax // ROW_ALIGN, dtype=I32) * ROW_ALIGN
    inrun = (trow[None, :, None] >= lo[:, None, :]) & (trow[None, :, None] < (lo + pc)[:, None, :])
    dst = trow[None, :] + jnp.sum(jnp.where(inrun, (go - lo)[:, None, :], 0), axis=2)
    i32 = lambda a: a.astype(I32)
    return dict(lo=i32(lo), dst=i32(dst), tot=i32(tot), tail_start=i32(gs + se), tail_len=i32(reg - se),
                nused=i32(nused), bexp=i32(bexp))


def _wait_rows(src_ref, dst_ref, sem, nrows):
    def wait_unit(unit):
        def body(i, c):
            pltpu.make_async_copy(src_ref.at[pl.ds(0, unit)], dst_ref.at[pl.ds(0, unit)], sem).wait()
            return c
        return body

    lax.fori_loop(0, nrows // SUB, wait_unit(SUB), 0)
    lax.fori_loop(0, (nrows % SUB) // LOCAL_ALIGN, wait_unit(LOCAL_ALIGN), 0)


def _start_tile_copies(copy_tile, dst_s, table_base, first_tile, rows):
    def body(q, c):
        for u in range(ISSUE_UNROLL):
            i = first_tile + q * ISSUE_UNROLL + u
            copy_tile(pl.multiple_of(i * ROW_ALIGN, ROW_ALIGN), pl.multiple_of(dst_s[table_base + i], ROW_ALIGN),
                      u % DMA_QUEUES)
        return c

    lax.fori_loop(0, rows // LOCAL_ALIGN, body, 0)


def _dispatch_kernel(dst_s, tot_s, tls_s, tll_s, nu_s,
                     e_ref, rk_ref, lov_ref, x_ref, xs_ref, slot_ref, xbuf, zbuf, sem, *, E, J, LT, nblk_max):
    j = pl.program_id(0)
    par = j % 2
    nt = x_ref.shape[0]

    eiota = lax.broadcasted_iota(I32, (E, nt), 0)
    for k in range(TOP_K):
        base = jnp.sum(jnp.where(eiota == e_ref[k:k + 1, :], lov_ref[...], 0), axis=0, keepdims=True)
        slot_ref[k:k + 1, :] = base + rk_ref[k:k + 1, :]
    slots16 = slot_ref[...].astype(jnp.int16)
    xb = x_ref[...]
    tot = tot_s[j]
    buf = xbuf.at[par]

    def permute_pair(c):
        for u in range(PAIR // SUB):
            base = pl.multiple_of(c * PAIR + u * SUB, SUB)
            siota = (lax.broadcasted_iota(I32, (SUB, nt), 0) + base).astype(jnp.int16)
            pb = jnp.zeros((SUB, nt), BF16)
            for k in range(TOP_K):
                pb = jnp.where(siota == slots16[k:k + 1, :], jnp.ones((), BF16), pb)
            buf[pl.ds(base, SUB), :] = jnp.dot(pb, xb, preferred_element_type=F32).astype(BF16)

    def copy_tile(local_row, sorted_row, queue):
        pltpu.make_async_copy(buf.at[pl.ds(local_row, ROW_ALIGN)], xs_ref.at[pl.ds(sorted_row, ROW_ALIGN)],
                              sem.at[par]).start(priority=queue)

    tiles_per_pair = PAIR // ROW_ALIGN

    def trip(c, carry):
        for i in range(tiles_per_pair):
            t = (c - 1) * tiles_per_pair + i
            copy_tile(pl.multiple_of(t * ROW_ALIGN, ROW_ALIGN), pl.multiple_of(dst_s[j * LT + t], ROW_ALIGN),
                      i % DMA_QUEUES)
        permute_pair(c)
        return carry

    npair = (tot + PAIR - 1) // PAIR
    permute_pair(0)
    lax.fori_loop(1, npair, trip, 0)
    _start_tile_copies(copy_tile, dst_s, j * LT, (npair - 1) * tiles_per_pair, tot - (npair - 1) * PAIR)

    ztile = zbuf.at[pl.ds(0, ROW_ALIGN)]

    @pl.when(j == 0)
    def _():
        zbuf[...] = jnp.zeros_like(zbuf)

        def per_expert_tail(e, carry):
            def per_tile(i, c):
                do = pl.multiple_of(tls_s[e] + i * ROW_ALIGN, ROW_ALIGN)
                pltpu.make_async_copy(ztile, xs_ref.at[pl.ds(do, ROW_ALIGN)], sem.at[2]).start()
                return c

            lax.fori_loop(0, tll_s[e] // ROW_ALIGN, per_tile, 0)
            return carry

        lax.fori_loop(0, E, per_expert_tail, 0)

        def per_block(b, c):
            do = pl.multiple_of(b * FFN_BLK, FFN_BLK)
            pltpu.make_async_copy(zbuf, xs_ref.at[pl.ds(do, FFN_BLK)], sem.at[2]).start()
            return c

        lax.fori_loop(nu_s[0], nblk_max, per_block, 0)

    @pl.when(j > 0)
    def _():
        _wait_rows(xbuf.at[1 - par], xs_ref, sem.at[1 - par], tot_s[j - 1])

    @pl.when(j == J - 1)
    def _():
        _wait_rows(buf, xs_ref, sem.at[par], tot)

        def wtail(i, c):
            pltpu.make_async_copy(ztile, xs_ref.at[pl.ds(0, ROW_ALIGN)], sem.at[2]).wait()
            return c

        def wblock(b, c):
            pltpu.make_async_copy(zbuf, xs_ref.at[pl.ds(0, FFN_BLK)], sem.at[2]).wait()
            return c

        ntile = lax.fori_loop(0, E, lambda e, n: n + tll_s[e] // ROW_ALIGN, 0)
        lax.fori_loop(0, ntile, wtail, 0)
        lax.fori_loop(nu_s[0], nblk_max, wblock, 0)


def _dispatch(tb, top_e, rank, x16, nblk_max, lmax):
    T, D = x16.shape
    J, E = tb["lo"].shape
    gs = pltpu.PrefetchScalarGridSpec(
        num_scalar_prefetch=5,
        grid=(J,),
        in_specs=[
            pl.BlockSpec((TOP_K, SUB), lambda j, *_: (0, j)),
            pl.BlockSpec((TOP_K, SUB), lambda j, *_: (0, j)),
            pl.BlockSpec((None, E, 1), lambda j, *_: (j, 0, 0)),
            pl.BlockSpec((SUB, D), lambda j, *_: (j, 0)),
        ],
        out_specs=(pl.BlockSpec(memory_space=pl.ANY), pl.BlockSpec((TOP_K, SUB), lambda j, *_: (0, j))),
        scratch_shapes=[
            pltpu.VMEM((2, lmax, D), BF16),
            pltpu.VMEM((FFN_BLK, D), BF16),
            pltpu.SemaphoreType.DMA((3,)),
        ],
    )
    return pl.pallas_call(
        functools.partial(_dispatch_kernel, E=E, J=J, LT=lmax // ROW_ALIGN, nblk_max=nblk_max),
        out_shape=(jax.ShapeDtypeStruct((nblk_max * FFN_BLK, D), BF16), jax.ShapeDtypeStruct((TOP_K, T), I32)),
        grid_spec=gs,
        compiler_params=_cparams(1),
        name="moe_dispatch",
    )(tb["dst"].reshape(-1), tb["tot"], tb["tail_start"], tb["tail_len"], tb["nused"],
      top_e, rank, tb["lo"].reshape(J, E, 1), x16)


def _ffn_kernel(be_s, nu_s, x_ref, w1_ref, w3_ref, w2_ref, y_ref):
    @pl.when(pl.program_id(0) < nu_s[0])
    def _():
        xb = x_ref[...]
        h1 = jnp.dot(xb, w1_ref[...].astype(BF16), preferred_element_type=F32)
        h3 = jnp.dot(xb, w3_ref[...].astype(BF16), preferred_element_type=F32)
        hh = (jax.nn.silu(h1) * h3).astype(BF16)
        y_ref[...] = jnp.dot(hh, w2_ref[...].astype(BF16), preferred_element_type=F32).astype(BF16)


def _expert_ffn(tb, xs, w1, w3, w2, layer, nblk_max):
    D = xs.shape[1]
    blk = lambda b, be, nu: jnp.minimum(b, nu[0] - 1)
    wspec = lambda shape: pl.BlockSpec((None, None) + shape, lambda b, be, nu: (layer, be[blk(b, be, nu)], 0, 0))
    gs = pltpu.PrefetchScalarGridSpec(
        num_scalar_prefetch=2,
        grid=(nblk_max,),
        in_specs=[
            pl.BlockSpec((FFN_BLK, D), lambda b, be, nu: (blk(b, be, nu), 0)),
            wspec((D, D_EXPERT)), wspec((D, D_EXPERT)), wspec((D_EXPERT, D)),
        ],
        out_specs=pl.BlockSpec((FFN_BLK, D), lambda b, be, nu: (blk(b, be, nu), 0)),
    )
    return pl.pallas_call(
        _ffn_kernel,
        out_shape=jax.ShapeDtypeStruct(xs.shape, BF16),
        grid_spec=gs,
        input_output_aliases={2: 0},
        compiler_params=_cparams(1),
        name="moe_expert_ffn",
    )(tb["bexp"], tb["nused"], xs, w1, w3, w2)


def _combine_kernel(dst_s, tot_s,
                    slot_ref, g_ref, x32_ref, x16_ref, ys_ref, sw13_ref, sw2_ref, lng_ref, lnb_ref,
                    o32_ref, o16_ref, ybuf, acc_ref, sem, *, J, LT, alpha):
    j = pl.program_id(0)
    par = j % 2
    nt = x32_ref.shape[0]
    tot = tot_s[j]
    buf = ybuf.at[par]

    def gather_tile(step, slot):
        def copy_tile(local_row, sorted_row, queue):
            pltpu.make_async_copy(ys_ref.at[pl.ds(sorted_row, ROW_ALIGN)],
                                  ybuf.at[slot, pl.ds(local_row, ROW_ALIGN)], sem.at[slot]).start(priority=queue)
        return copy_tile

    @pl.when(j == 0)
    def _():
        _start_tile_copies(gather_tile(0, 0), dst_s, 0, 0, tot_s[0])

    npair = (tot + PAIR - 1) // PAIR

    def zrows(i, c):
        zo = pl.multiple_of(tot + i * LOCAL_ALIGN, LOCAL_ALIGN)
        buf[pl.ds(zo, LOCAL_ALIGN), :] = jnp.zeros((LOCAL_ALIGN, ybuf.shape[2]), BF16)
        return c

    lax.fori_loop(0, (npair * PAIR - tot) // LOCAL_ALIGN, zrows, 0)

    xb = x16_ref[...]
    h = jnp.dot(xb, sw13_ref[...], preferred_element_type=F32)
    hh = (jax.nn.silu(h[:, :D_EXPERT]) * h[:, D_EXPERT:]).astype(BF16)
    acc_ref[...] = jnp.dot(hh, sw2_ref[...], preferred_element_type=F32)

    _wait_rows(ys_ref, buf, sem.at[par], tot)

    slots16 = slot_ref[...].astype(jnp.int16)
    gates16 = g_ref[...].astype(BF16)
    tiles_per_pair = PAIR // ROW_ALIGN
    copy_next = gather_tile(j + 1, 1 - par)

    def pair(c, carry, prefetch=False):
        if prefetch:
            for i in range(tiles_per_pair):
                t = c * tiles_per_pair + i
                copy_next(pl.multiple_of(t * ROW_ALIGN, ROW_ALIGN),
                          pl.multiple_of(dst_s[(j + 1) * LT + t], ROW_ALIGN), i % DMA_QUEUES)
        part = None
        for u in range(PAIR // SUB):
            base = pl.multiple_of(c * PAIR + u * SUB, SUB)
            siota = (lax.broadcasted_iota(I32, (SUB, nt), 0) + base).astype(jnp.int16)
            wt = jnp.zeros((SUB, nt), BF16)
            for k in range(TOP_K):
                wt = jnp.where(siota == slots16[k:k + 1, :], gates16[k:k + 1, :], wt)
            y = lax.dot_general(wt, buf[pl.ds(base, SUB), :], (((0,), (0,)), ((), ())),
                                preferred_element_type=F32)
            part = y if part is None else part + y
        acc_ref[...] += part
        return carry

    sure = SUB * TOP_K // PAIR

    @pl.when(j + 1 < J)
    def _():
        lax.fori_loop(0, sure, functools.partial(pair, prefetch=True), 0)
        _start_tile_copies(copy_next, dst_s, (j + 1) * LT, sure * tiles_per_pair, tot_s[j + 1] - sure * PAIR)

    @pl.when(j + 1 >= J)
    def _():
        lax.fori_loop(0, sure, pair, 0)

    lax.fori_loop(sure, npair, pair, 0)

    z = _layer_norm(alpha * x32_ref[...] + acc_ref[...], lng_ref[...], lnb_ref[...])
    o32_ref[...] = z
    o16_ref[...] = z.astype(BF16)


def _combine(tb, slot, gate, x32, x16, ys, sw13, sw2, ln_g, ln_b, alpha, lmax):
    T, D = x32.shape
    J = T // SUB
    full = lambda shape: pl.BlockSpec(shape, lambda j, *_: (0,) * len(shape))
    gs = pltpu.PrefetchScalarGridSpec(
        num_scalar_prefetch=2,
        grid=(J,),
        in_specs=[
            pl.BlockSpec((TOP_K, SUB), lambda j, *_: (0, j)),
            pl.BlockSpec((TOP_K, SUB), lambda j, *_: (0, j)),
            pl.BlockSpec((SUB, D), lambda j, *_: (j, 0)),
            pl.BlockSpec((SUB, D), lambda j, *_: (j, 0)),
            pl.BlockSpec(memory_space=pl.ANY),
            full(sw13.shape), full(sw2.shape), full((1, D)), full((1, D)),
        ],
        out_specs=(pl.BlockSpec((SUB, D), lambda j, *_: (j, 0)), pl.BlockSpec((SUB, D), lambda j, *_: (j, 0))),
        scratch_shapes=[
            pltpu.VMEM((2, lmax, D), BF16),
            pltpu.VMEM((SUB, D), F32),
            pltpu.SemaphoreType.DMA((2,)),
        ],
    )
    return pl.pallas_call(
        functools.partial(_combine_kernel, J=J, LT=lmax // ROW_ALIGN, alpha=alpha),
        out_shape=(jax.ShapeDtypeStruct((T, D), F32), jax.ShapeDtypeStruct((T, D), BF16)),
        grid_spec=gs,
        compiler_params=_cparams(1),
        name="moe_combine",
    )(tb["dst"].reshape(-1), tb["tot"], slot, gate, x32, x16, ys, sw13, sw2, ln_g, ln_b)


def kernel(x, positions, w_in, gate_b, w_branch, w_out, pool_w, pool_scale, rg_conv_w, rg_conv_b, rg_wa, rg_ba,
           rg_wx, rg_bx, rg_lambda, ln1_g, ln1_b, router_w, router_bias, exp_w1, exp_w3, exp_w2, sh_w1, sh_w3,
           sh_w2, ln2_g, ln2_b):
    B, S, D = x.shape
    L = w_in.shape[0]
    T = B * S
    E = router_w.shape[2]
    assert T % SUB == 0 and S % CHUNK == 0 and E == N_GROUPS * E_PER_GROUP
    alpha = (2 * L) ** 0.25
    n_a = 4 * RET_W + POOL_W + 2 * RG_W
    nblk_max = _num_blocks_max(T, E)
    lmax = _local_rows_max(E)

    cos, sin = _rope_tables(positions)
    consts = _retention_consts()
    tri = (jnp.arange(SUB)[:, None] < jnp.arange(SUB)[None, :]).astype(BF16)
    row = lambda a: a.reshape(1, -1)

    x32 = x.reshape(T, D)
    x16 = x32.astype(BF16)
    for l in range(L):
        ha = _in_proj(x16, w_in, l, 0, n_a, "in_proj_a")
        gl = _in_proj(x16, w_in, l, n_a, w_in.shape[2] - n_a, "in_proj_gates")
        br = _branches(ha, cos, sin, consts, pool_w[l].astype(BF16), row(pool_scale[l]), rg_conv_w[l],
                       row(rg_conv_b[l]), rg_wa[l].astype(BF16), row(rg_ba[l]), rg_wx[l].astype(BF16),
                       row(rg_bx[l]), row(rg_lambda[l]), B, S)
        x32, x16 = _merge(br, gl, x32, w_branch[l].astype(BF16), w_out[l].astype(BF16), gate_b[l],
                          row(ln1_g[l]), row(ln1_b[l]), alpha)

        top_e, gate, rank, cnt = _router(x32, x16, router_w[l].T, router_bias[l].reshape(E, 1), tri)
        tb = _tables(cnt[:, :, 0], nblk_max, lmax)
        xs, slot = _dispatch(tb, top_e, rank, x16, nblk_max, lmax)
        ys = _expert_ffn(tb, xs, exp_w1, exp_w3, exp_w2, l, nblk_max)
        sw13 = jnp.concatenate([sh_w1[l], sh_w3[l]], axis=-1).astype(BF16)
        x32, x16 = _combine(tb, slot, gate, x32, x16, ys, sw13, sh_w2[l].astype(BF16),
                            row(ln2_g[l]), row(ln2_b[l]), alpha, lmax)
    return x32.reshape(B, S, D)
```
